```python
import jax, jax.numpy as jnp
from jax import lax
import numpy as np

D_MODEL = 1024
BATCH = 8
SEQ = 4096
DEPTH = 1

N_META = 16
NORM_EPS = 1e-5
GLA_HEADS = 4
GLA_KEY = D_MODEL // 2
GLA_VAL = D_MODEL
GLA_DK = GLA_KEY // GLA_HEADS
GLA_DV = GLA_VAL // GLA_HEADS
GLA_GATE_RANK = 16
GLA_GATE_NORM = 16.0
GLA_CHUNK = 64
RWKV_HEAD = 64
RWKV_WIDTH = D_MODEL
RWKV_HEADS = RWKV_WIDTH // RWKV_HEAD
RWKV_DECAY_RANK = 64
RWKV_A_RANK = 64
RWKV_GATE_RANK = 160
RWKV_GN_EPS = 64e-5
N_EXPERTS = 32
TOP_K = 4
D_EXPERT = D_MODEL
SWIGLU_LIMIT = 7.0
SWIGLU_ALPHA = 1.702
IN_SPLITS = (GLA_KEY, GLA_KEY, GLA_VAL, GLA_VAL, GLA_GATE_RANK,
             RWKV_WIDTH, RWKV_WIDTH, RWKV_WIDTH, RWKV_DECAY_RANK, RWKV_A_RANK, RWKV_GATE_RANK,
             D_MODEL, D_MODEL)
D_IN = sum(IN_SPLITS)

kernel_name = 'hybrid_gla_rwkv7_moe_block'


def rmsnorm(x, g):
    xf = x.astype(jnp.float32)
    y = xf * lax.rsqrt(jnp.mean(xf * xf, axis=-1, keepdims=True) + NORM_EPS)
    return (y * g).astype(x.dtype)


def token_shift(z):
    return jnp.pad(z, ((0, 0), (1, 0), (0, 0)))[:, :-1]


def gla_branch(q, k, v, r, a_low, w_gla_a2, b_gla_a, gla_norm_g):
    B, L, _ = q.shape
    f32 = jnp.float32
    C = GLA_CHUNK
    pad = (-N_META) % C
    gk = jax.nn.log_sigmoid((a_low @ w_gla_a2 + b_gla_a).astype(f32)) / GLA_GATE_NORM

    def chunks(z, dh):
        z = jnp.pad(z.astype(f32), ((0, 0), (pad, 0), (0, 0)))
        return z.reshape(B, -1, C, GLA_HEADS, dh).transpose(0, 3, 1, 2, 4)

    qc = chunks(q, GLA_DK) * GLA_DK ** -0.5
    kc = chunks(k, GLA_DK)
    vc = chunks(v, GLA_DV)
    b = jnp.cumsum(chunks(gk, GLA_DK), axis=3)
    b_ref = b[:, :, :, C // 2 - 1:C // 2]
    A = jnp.einsum('bhnid,bhnjd->bhnij', qc * jnp.exp(b - b_ref), kc * jnp.exp(b_ref - b))
    A = jnp.where(jnp.tril(jnp.ones((C, C), dtype=bool)), A, 0.0)
    o_intra = jnp.einsum('bhnij,bhnjv->bhniv', A, vc)
    b_last = b[:, :, :, -1:]
    chunk_kv = jnp.einsum('bhncd,bhncv->bhndv', kc * jnp.exp(b_last - b), vc)
    chunk_decay = jnp.exp(b_last[:, :, :, 0])

    def step(S, inp):
        d, kv = inp
        return d[..., None] * S + kv, S

    S0 = jnp.zeros((B, GLA_HEADS, GLA_DK, GLA_DV), f32)
    _, S_start = lax.scan(step, S0, (jnp.moveaxis(chunk_decay, 2, 0), jnp.moveaxis(chunk_kv, 2, 0)))
    S_start = jnp.moveaxis(S_start, 0, 2)
    o_inter = jnp.einsum('bhncd,bhndv->bhncv', qc * jnp.exp(b), S_start)
    o = (o_intra + o_inter).transpose(0, 2, 3, 1, 4).reshape(B, -1, GLA_HEADS, GLA_DV)[:, pad:]
    o = o * lax.rsqrt(jnp.mean(o * o, axis=-1, keepdims=True) + NORM_EPS) * gla_norm_g
    o = o.reshape(B, L, GLA_VAL) * jax.nn.silu(r.astype(f32))
    return o.astype(q.dtype)


def rwkv7_branch(pr, pk, pv, pw, pa, pg, mu_r, mu_k, mu_v, mu_w, mu_a, mu_g,
                 w_decay2, b_decay, w_a2, b_a, w_gate2, k_k, k_a, r_k, ln_x_g, ln_x_b):
    B, L, _ = pr.shape
    f32 = jnp.float32
    lerp = lambda z, mu: (z + (token_shift(z) - z) * mu).astype(f32)
    r = lerp(pr, mu_r)
    k = lerp(pk, mu_k)
    v = lerp(pv, mu_v)
    w_log = -jax.nn.softplus(-(b_decay + jnp.tanh(lerp(pw, mu_w)) @ w_decay2)) - 0.5
    decay = jnp.exp(-jnp.exp(w_log))
    a = jax.nn.sigmoid(b_a + lerp(pa, mu_a) @ w_a2)
    g = jax.nn.sigmoid(lerp(pg, mu_g)) @ w_gate2
    heads = lambda z: z.reshape(B, L, RWKV_HEADS, RWKV_HEAD)
    r, k, v, decay, a = heads(r), heads(k), heads(v), heads(decay), heads(a)
    kk = k * k_k.reshape(RWKV_HEADS, RWKV_HEAD)
    kk = kk / jnp.maximum(jnp.sqrt(jnp.sum(kk * kk, axis=-1, keepdims=True)), 1e-12)
    k = k * (1.0 + (a - 1.0) * k_a.reshape(RWKV_HEADS, RWKV_HEAD))

    def step(S, inp):
        r_t, w_t, k_t, v_t, a_t, b_t = inp
        sa = jnp.einsum('bhij,bhj->bhi', S, a_t)
        S = S * w_t[:, :, None, :] + sa[..., None] * b_t[:, :, None, :] + v_t[..., None] * k_t[:, :, None, :]
        return S, jnp.einsum('bhij,bhj->bhi', S, r_t)

    tm = lambda z: jnp.moveaxis(z, 1, 0)
    S0 = jnp.zeros((B, RWKV_HEADS, RWKV_HEAD, RWKV_HEAD), f32)
    _, ys = lax.scan(step, S0, (tm(r), tm(decay), tm(k), tm(v), tm(-kk), tm(kk * a)))
    y = jnp.moveaxis(ys, 0, 1)
    mean = jnp.mean(y, axis=-1, keepdims=True)
    var = jnp.mean(jnp.square(y - mean), axis=-1, keepdims=True)
    y = (y - mean) * lax.rsqrt(var + RWKV_GN_EPS)
    y = y * ln_x_g.reshape(RWKV_HEADS, RWKV_HEAD) + ln_x_b.reshape(RWKV_HEADS, RWKV_HEAD)
    y = y + jnp.sum(r * k * r_k, axis=-1, keepdims=True) * v
    return (y.reshape(B, L, RWKV_WIDTH) * g).astype(pr.dtype)


def moe_ffn(u, w_router, b_router, w_exp_gu, b_exp_gu, w_exp_down, b_exp_down):
    B, L, D = u.shape
    ut = u.reshape(B * L, D)
    logits = (ut @ w_router + b_router).astype(jnp.float32)
    top_vals, top_idx = lax.top_k(logits, TOP_K)
    probs = jax.nn.softmax(top_vals, axis=-1)
    comb = jnp.sum(jax.nn.one_hot(top_idx, N_EXPERTS, dtype=jnp.float32) * probs[..., None], axis=1)
    comb = comb.astype(ut.dtype)

    def expert(acc, p):
        w_gu, b_gu, w_dn, b_dn, c_e = p
        h = ut @ w_gu + b_gu
        gate, up = h[:, :D_EXPERT], h[:, D_EXPERT:]
        gate = jnp.minimum(gate, SWIGLU_LIMIT)
        up = jnp.clip(up, -SWIGLU_LIMIT, SWIGLU_LIMIT)
        out = ((up + 1.0) * (gate * jax.nn.sigmoid(gate * SWIGLU_ALPHA))) @ w_dn + b_dn
        return acc + c_e[:, None] * out, None

    acc, _ = lax.scan(expert, jnp.zeros_like(ut), (w_exp_gu, b_exp_gu, w_exp_down, b_exp_down, comb.T))
    return acc.reshape(B, L, D)


def setup_inputs(seed: int = 0) -> dict:
    key = jax.random.key(seed)
    ks = list(jax.random.split(key, 48))
    it = iter(ks)
    D, Ly = D_MODEL, DEPTH
    nrm = lambda shape, s: jax.random.normal(next(it), shape, jnp.float32) * s
    uni = lambda shape, lo, hi: jax.random.uniform(next(it), shape, jnp.float32, lo, hi)
    gain = lambda shape: 1.0 + nrm(shape, 0.02)
    return {
        'x': nrm((BATCH, SEQ, D), 1.0),
        'meta_tokens': nrm((N_META, D), 1.0),
        'norm_mix_g': gain((Ly, D)),
        'w_in': nrm((Ly, D, D_IN), D ** -0.5),
        'w_gla_a2': nrm((Ly, GLA_GATE_RANK, GLA_KEY), GLA_GATE_RANK ** -0.5),
        'b_gla_a': nrm((Ly, GLA_KEY), 1.0),
        'gla_norm_g': gain((Ly, GLA_DV)),
        'w_gla_o': nrm((Ly, GLA_VAL, D), GLA_VAL ** -0.5),
        'mu_r': uni((Ly, RWKV_WIDTH), 0.0, 1.0),
        'mu_k': uni((Ly, RWKV_WIDTH), 0.0, 1.0),
        'mu_v': uni((Ly, RWKV_WIDTH), 0.0, 1.0),
        'mu_w': uni((Ly, RWKV_DECAY_RANK), 0.0, 1.0),
        'mu_a': uni((Ly, RWKV_A_RANK), 0.0, 1.0),
        'mu_g': uni((Ly, RWKV_GATE_RANK), 0.0, 1.0),
        'w_decay2': nrm((Ly, RWKV_DECAY_RANK, RWKV_WIDTH), RWKV_DECAY_RANK ** -0.5),
        'b_decay': uni((Ly, RWKV_WIDTH), -5.0, 1.0),
        'w_a2': nrm((Ly, RWKV_A_RANK, RWKV_WIDTH), RWKV_A_RANK ** -0.5),
        'b_a': nrm((Ly, RWKV_WIDTH), 0.1),
        'w_gate2': nrm((Ly, RWKV_GATE_RANK, RWKV_WIDTH), RWKV_GATE_RANK ** -0.5),
        'k_k': 0.85 + nrm((Ly, RWKV_WIDTH), 0.02),
        'k_a': gain((Ly, RWKV_WIDTH)),
        'r_k': nrm((Ly, RWKV_HEADS, RWKV_HEAD), 0.1),
        'ln_x_g': gain((Ly, RWKV_WIDTH)),
        'ln_x_b': nrm((Ly, RWKV_WIDTH), 0.01),
        'w_rwkv_o': nrm((Ly, RWKV_WIDTH, D), RWKV_WIDTH ** -0.5),
        'w_out': nrm((Ly, D, D), D ** -0.5),
        'norm_ffn_g': gain((Ly, D)),
        'w_router': nrm((Ly, D, N_EXPERTS), D ** -0.5),
        'b_router': nrm((Ly, N_EXPERTS), 0.01),
        'w_exp_gu': nrm((Ly, N_EXPERTS, D, 2 * D_EXPERT), D ** -0.5),
        'b_exp_gu': nrm((Ly, N_EXPERTS, 2 * D_EXPERT), 0.01),
        'w_exp_down': nrm((Ly, N_EXPERTS, D_EXPERT, D), D_EXPERT ** -0.5),
        'b_exp_down': nrm((Ly, N_EXPERTS, D), 0.01),
        'norm_final_g': gain((D,)),
    }


def reference(x, meta_tokens, norm_mix_g, w_in, w_gla_a2, b_gla_a, gla_norm_g, w_gla_o,
              mu_r, mu_k, mu_v, mu_w, mu_a, mu_g, w_decay2, b_decay, w_a2, b_a, w_gate2,
              k_k, k_a, r_k, ln_x_g, ln_x_b, w_rwkv_o, w_out, norm_ffn_g,
              w_router, b_router, w_exp_gu, b_exp_gu, w_exp_down, b_exp_down, norm_final_g):
    B = x.shape[0]
    meta = jnp.broadcast_to(meta_tokens[None].astype(x.dtype), (B, N_META, D_MODEL))
    h = jnp.concatenate([meta, x], axis=1)
    offsets = [int(o) for o in np.cumsum(IN_SPLITS)[:-1]]
    for l in range(DEPTH):
        u = rmsnorm(h, norm_mix_g[l])
        P = u @ w_in[l]
        (gq, gk_, gv, gr, ga, rr, rk, rv, rw, ra, rg, gate_gla, gate_rwkv) = jnp.split(P, offsets, axis=-1)
        o_gla = gla_branch(gq, gk_, gv, gr, ga, w_gla_a2[l], b_gla_a[l], gla_norm_g[l]) @ w_gla_o[l]
        o_rwkv = rwkv7_branch(rr, rk, rv, rw, ra, rg, mu_r[l], mu_k[l], mu_v[l], mu_w[l], mu_a[l], mu_g[l],
                              w_decay2[l], b_decay[l], w_a2[l], b_a[l], w_gate2[l], k_k[l], k_a[l], r_k[l],
                              ln_x_g[l], ln_x_b[l]) @ w_rwkv_o[l]
        mix = jax.nn.sigmoid(gate_gla) * o_gla + jax.nn.sigmoid(gate_rwkv) * o_rwkv
        h = h + mix @ w_out[l]
        h = h + moe_ffn(rmsnorm(h, norm_ffn_g[l]), w_router[l], b_router[l],
                        w_exp_gu[l], b_exp_gu[l], w_exp_down[l], b_exp_down[l])
    return rmsnorm(h, norm_final_g)[:, N_META:]
```

```python
import functools

import jax
import jax.numpy as jnp
from jax import lax
from jax.experimental import pallas as pl
from jax.experimental.pallas import tpu as pltpu

F32 = jnp.float32
BF16 = jnp.bfloat16

D_MODEL = 1024
N_META = 16
NORM_EPS = 1e-5
CHUNK = 64
FRONT_PAD = (-N_META) % CHUNK
GLA_HEADS = 4
GLA_DK = 128
GLA_DV = 256
GLA_KEY = GLA_HEADS * GLA_DK
GLA_GATE_RANK = 16
GLA_GATE_NORM = 16.0
RWKV_HEAD = 64
RWKV_PAIRS = D_MODEL // (2 * RWKV_HEAD)
RWKV_DECAY_RANK = 64
RWKV_A_RANK = 64
RWKV_GATE_RANK = 160
RWKV_GN_EPS = 64e-5
N_EXPERTS = 32
TOP_K = 4
SWIGLU_LIMIT = 7.0
SWIGLU_ALPHA = 1.702
LANES = 128
SMALL_W = 512
COL_GQ, COL_GK, COL_GV, COL_GR = 0, 512, 1024, 2048
COL_RR, COL_RK, COL_RV = 3072, 4096, 5120
COL_GATE_GLA, COL_GATE_RWKV, COL_SMALL = 6144, 7168, 8192
NP_COLS = COL_SMALL + SMALL_W
VMEM_LIMIT = 56 * 1024 * 1024


def _dot(a, b):
    return jnp.dot(a, b, preferred_element_type=F32)


def _dot_nt(a, b):
    return lax.dot_general(a, b, (((1,), (1,)), ((), ())), preferred_element_type=F32)


def _dot_tn(a, b):
    return lax.dot_general(a, b, (((0,), (0,)), ((), ())), preferred_element_type=F32)


def _split2(x):
    hi = x.astype(BF16)
    lo = (x - hi.astype(F32)).astype(BF16)
    return hi, lo


def _split3(x):
    hi = x.astype(BF16)
    r1 = x - hi.astype(F32)
    mid = r1.astype(BF16)
    lo = (r1 - mid.astype(F32)).astype(BF16)
    return hi, mid, lo


def _cumsum_rows(x, tri):
    hi, mid, lo = _split3(x)
    return _dot(tri, hi) + _dot(tri, mid) + _dot(tri, lo)


def _softplus(x):
    return jnp.maximum(x, 0.0) + jnp.log1p(jnp.exp(-jnp.abs(x)))


def _tri_incl(n):
    r = lax.broadcasted_iota(jnp.int32, (n, n), 0)
    c = lax.broadcasted_iota(jnp.int32, (n, n), 1)
    return (r >= c).astype(BF16)


def _inproj_body(x_ref, g_ref, w_ref, o_ref, u_scr):
    @pl.when(pl.program_id(1) == 0)
    def _():
        x = x_ref[...]
        ms = jnp.mean(x * x, axis=-1, keepdims=True)
        u_scr[...] = (x * lax.rsqrt(ms + NORM_EPS) * g_ref[...]).astype(BF16)

    o_ref[...] = _dot(u_scr[...], w_ref[...])


def _inproj(hp, g, w, tm, tn):
    m = hp.shape[0]
    return pl.pallas_call(
        _inproj_body,
        grid=(m // tm, NP_COLS // tn),
        in_specs=[
            pl.BlockSpec((tm, D_MODEL), lambda i, j: (i, 0)),
            pl.BlockSpec((1, D_MODEL), lambda i, j: (0, 0)),
            pl.BlockSpec((D_MODEL, tn), lambda i, j: (0, j)),
        ],
        out_specs=pl.BlockSpec((tm, tn), lambda i, j: (i, j)),
        out_shape=jax.ShapeDtypeStruct((m, NP_COLS), F32),
        scratch_shapes=[pltpu.VMEM((tm, D_MODEL), BF16)],
        compiler_params=pltpu.CompilerParams(
            dimension_semantics=("parallel", "arbitrary"), vmem_limit_bytes=VMEM_LIMIT),
        name="inproj",
    )(hp, g, w)


def _gla_body(q_ref, k_ref, v_ref, r_ref, sm_ref, gate_ref, wa2_ref, ba_ref, ng_ref, wo_ref,
              o_ref, s_scr):
    c = pl.program_id(1)

    @pl.when(c == 0)
    def _():
        s_scr[...] = jnp.zeros_like(s_scr)

    tri = _tri_incl(CHUNK)
    row = lax.broadcasted_iota(jnp.int32, (CHUNK, 1), 0)
    col = lax.broadcasted_iota(jnp.int32, (CHUNK, CHUNK), 1)
    causal = lax.broadcasted_iota(jnp.int32, (CHUNK, CHUNK), 0) >= col

    z = _dot(sm_ref[:, 0:LANES].astype(BF16), wa2_ref[...]) + ba_ref[...]
    gk = -_softplus(-z) * (1.0 / GLA_GATE_NORM)
    gk = jnp.where((c > 0) | (row >= FRONT_PAD), gk, 0.0)
    b = _cumsum_rows(gk, tri)
    b_ref = b[CHUNK // 2 - 1:CHUNK // 2, :]
    b_last = b[CHUNK - 1:CHUNK, :]

    q = q_ref[...] * (GLA_DK ** -0.5)
    k = k_ref[...]
    qe = (q * jnp.exp(b - b_ref)).astype(BF16)
    ke = (k * jnp.exp(b_ref - b)).astype(BF16)
    qs = (q * jnp.exp(b)).astype(BF16)
    kd = (k * jnp.exp(b_last - b)).astype(BF16)
    decay = jnp.exp(b_last)
    v = v_ref[...].astype(BF16)
    silu_r = r_ref[...]
    silu_r = silu_r * jax.nn.sigmoid(silu_r)

    outs = []
    for h in range(GLA_HEADS):
        ks = slice(h * GLA_DK, (h + 1) * GLA_DK)
        vs = slice(h * GLA_DV, (h + 1) * GLA_DV)
        a = jnp.where(causal, _dot_nt(qe[:, ks], ke[:, ks]), 0.0)
        st = s_scr[h]
        o = _dot(a.astype(BF16), v[:, vs]) + _dot_nt(qs[:, ks], st.astype(BF16))
        s_scr[h] = st * decay[:, ks] + _dot_tn(v[:, vs], kd[:, ks])
        o = o * lax.rsqrt(jnp.mean(o * o, axis=-1, keepdims=True) + NORM_EPS) * ng_ref[...]
        outs.append(o * silu_r[:, vs])
    og = jnp.concatenate(outs, axis=1).astype(BF16)
    o_ref[...] = jax.nn.sigmoid(gate_ref[...]) * _dot(og, wo_ref[...])


def _gla(p, wa2, ba, ng, wo, batch, n_chunks):
    rows = lambda b, c: b * n_chunks + c
    pspec = lambda w, col: pl.BlockSpec((CHUNK, w), lambda b, c: (rows(b, c), col // w))
    full = lambda shape: pl.BlockSpec(shape, lambda b, c: (0,) * len(shape))
    n_real = n_chunks - 1
    return pl.pallas_call(
        _gla_body,
        grid=(batch, n_chunks),
        in_specs=[
            pspec(GLA_KEY, COL_GQ), pspec(GLA_KEY, COL_GK), pspec(D_MODEL, COL_GV),
            pspec(D_MODEL, COL_GR), pspec(SMALL_W, COL_SMALL), pspec(D_MODEL, COL_GATE_GLA),
            full((LANES, GLA_KEY)), full((1, GLA_KEY)), full((1, GLA_DV)), full((D_MODEL, D_MODEL)),
        ],
        out_specs=pl.BlockSpec((CHUNK, D_MODEL), lambda b, c: (b * n_real + jnp.maximum(c - 1, 0), 0)),
        out_shape=jax.ShapeDtypeStruct((batch * n_real * CHUNK, D_MODEL), F32),
        scratch_shapes=[pltpu.VMEM((GLA_HEADS, GLA_DV, GLA_DK), F32)],
        compiler_params=pltpu.CompilerParams(
            dimension_semantics=("parallel", "arbitrary"), vmem_limit_bytes=VMEM_LIMIT),
        name="gla",
    )(p, p, p, p, p, p, wa2, ba, ng, wo)


def _stack_heads(x, lane_lo):
    return jnp.concatenate([jnp.where(lane_lo, x, 0.0), jnp.where(lane_lo, 0.0, x)], axis=0)


def _rwkv_body(pr_ref, pk_ref, pv_ref, sm_ref, gate_ref, mg_ref, x_ref,
               mur_ref, muk_ref, muv_ref, musm_ref, wd_ref, bd_ref, wa_ref, ba_ref, wg_ref,
               kk_ref, ka_ref, rk_ref, lng_ref, lnb_ref, wo_ref, wout_ref,
               o_ref,
               shr_scr, shk_scr, shv_scr, shs_scr, r_scr, k_scr, v_scr, lw_scr, a_scr, y_scr, s_scr):
    c = pl.program_id(1)
    C = CHUNK

    @pl.when(c == 0)
    def _():
        s_scr[...] = jnp.zeros_like(s_scr)
        for scr in (shr_scr, shk_scr, shv_scr, shs_scr):
            scr[7:8, :] = jnp.zeros((1, scr.shape[1]), F32)

    def lerp(x_ref_, scr, mu_ref):
        x = x_ref_[...]
        scr[8:8 + C, :] = x
        prev = scr[7:7 + C, :]
        scr[7:8, :] = x[C - 1:C, :]
        return x + (prev - x) * mu_ref[...]

    r = lerp(pr_ref, shr_scr, mur_ref)
    k = lerp(pk_ref, shk_scr, muk_ref)
    v = lerp(pv_ref, shv_scr, muv_ref)
    ls = lerp(sm_ref, shs_scr, musm_ref)
    s1 = ls[:, LANES:2 * LANES]
    xw = bd_ref[...] + _dot(jnp.tanh(s1).astype(BF16), wd_ref[...])
    logw = -jnp.exp(-_softplus(-xw) - 0.5)
    a = jax.nn.sigmoid(ba_ref[...] + _dot(s1.astype(BF16), wa_ref[...]))
    g = _dot(jax.nn.sigmoid(ls[:, 2 * LANES:4 * LANES]).astype(BF16), wg_ref[...])
    for p in range(RWKV_PAIRS):
        sl = slice(p * LANES, (p + 1) * LANES)
        r_scr[p] = r[:, sl]
        k_scr[p] = k[:, sl]
        v_scr[p] = v[:, sl]
        lw_scr[p] = logw[:, sl]
        a_scr[p] = a[:, sl]

    tri = _tri_incl(C)
    lane_lo = lax.broadcasted_iota(jnp.int32, (C, LANES), 1) < RWKV_HEAD
    er = lax.broadcasted_iota(jnp.int32, (LANES, LANES), 0)
    ec = lax.broadcasted_iota(jnp.int32, (LANES, LANES), 1)
    seg_ones = ((er // RWKV_HEAD) == (ec // RWKV_HEAD)).astype(BF16)
    t_idx = er % C
    s_idx = ec % C
    strict = t_idx > s_idx
    incl = t_idx >= s_idx

    def seg(x):
        hi, lo = _split2(x)
        return _dot(hi, seg_ones) + _dot(lo, seg_ones)

    def pair(p, carry):
        r = r_scr[p]
        k = k_scr[p]
        v = v_scr[p]
        lw = lw_scr[p]
        a = a_scr[p]
        kk = k * kk_ref[p]
        kk = kk / jnp.maximum(jnp.sqrt(seg(kk * kk)), 1e-12)
        k2 = k * (1.0 + (a - 1.0) * ka_ref[p])
        av = -kk
        bv = kk * a

        cl = _cumsum_rows(lw, tri)
        cref = cl[C // 2 - 1:C // 2, :]
        clast = cl[C - 1:C, :]
        e_neg = jnp.exp(cref - cl)
        at = av * jnp.exp(cl - lw - cref)
        rt = r * jnp.exp(cl - cref)
        bt = bv * e_neg
        kt = k2 * e_neg
        g_last = jnp.exp(clast - cref)
        st = s_scr[p]
        sp = (st * jnp.exp(cref)).astype(BF16)

        lhs = jnp.concatenate([_stack_heads(at, lane_lo), _stack_heads(rt, lane_lo)], axis=0).astype(BF16)
        rhs = jnp.concatenate([_stack_heads(bt, lane_lo), _stack_heads(kt, lane_lo)], axis=0).astype(BF16)
        m1 = _dot_nt(lhs, rhs)
        n2 = 2 * C
        ab = jnp.where(strict, m1[:n2, :n2], 0.0).astype(BF16)
        ak = jnp.where(strict, m1[:n2, n2:], 0.0).astype(BF16)
        rb = jnp.where(incl, m1[n2:, :n2], 0.0).astype(BF16)
        rk = jnp.where(incl, m1[n2:, n2:], 0.0).astype(BF16)
        vs = _stack_heads(v, lane_lo).astype(BF16)

        pm = _dot_nt(lhs[:n2], sp) + _dot(ak, vs)
        x = ab
        for i in range(6):
            pm = pm + _dot(x, pm.astype(BF16))
            if i < 5:
                x = _dot(x, x).astype(BF16)
        pmb = pm.astype(BF16)
        y2 = _dot_nt(lhs[n2:], sp) + _dot(rb, pmb) + _dot(rk, vs)
        y = y2[:C] + y2[C:]

        upd_l = jnp.concatenate([pmb, vs], axis=0)
        upd_r = jnp.concatenate([_stack_heads(bt * g_last, lane_lo),
                                 _stack_heads(kt * g_last, lane_lo)], axis=0).astype(BF16)
        s_scr[p] = st * jnp.exp(clast) + _dot_tn(upd_l, upd_r)

        mean = seg(y) * (1.0 / RWKV_HEAD)
        yc = y - mean
        var = seg(yc * yc) * (1.0 / RWKV_HEAD)
        yn = yc * lax.rsqrt(var + RWKV_GN_EPS) * lng_ref[p] + lnb_ref[p]
        y_scr[p] = yn + seg(r * k2 * rk_ref[p]) * v
        return carry

    lax.fori_loop(0, RWKV_PAIRS, pair, 0)

    y = jnp.concatenate([y_scr[p] for p in range(RWKV_PAIRS)], axis=1)
    orw = _dot((y * g).astype(BF16), wo_ref[...])
    mix = mg_ref[...] + jax.nn.sigmoid(gate_ref[...]) * orw
    o_ref[...] = x_ref[...] + _dot(mix.astype(BF16), wout_ref[...])


def _rwkv(p, mg, x2, prm, batch, n_chunks):
    rows = lambda b, c: b * n_chunks + c
    n_real = n_chunks - 1
    real = lambda b, c: (b * n_real + jnp.maximum(c - 1, 0), 0)
    pspec = lambda w, col: pl.BlockSpec((CHUNK, w), lambda b, c: (rows(b, c), col // w))
    full = lambda shape: pl.BlockSpec(shape, lambda b, c: (0,) * len(shape))
    pairvec = full((RWKV_PAIRS, 1, LANES))
    vec = full((1, D_MODEL))
    pair_scr = pltpu.VMEM((RWKV_PAIRS, CHUNK, LANES), F32)
    return pl.pallas_call(
        _rwkv_body,
        grid=(batch, n_chunks),
        in_specs=[
            pspec(D_MODEL, COL_RR), pspec(D_MODEL, COL_RK), pspec(D_MODEL, COL_RV),
            pspec(SMALL_W, COL_SMALL), pspec(D_MODEL, COL_GATE_RWKV),
            pl.BlockSpec((CHUNK, D_MODEL), real), pl.BlockSpec((CHUNK, D_MODEL), real),
            vec, vec, vec, full((1, SMALL_W)),
            full((LANES, D_MODEL)), vec, full((LANES, D_MODEL)), vec, full((2 * LANES, D_MODEL)),
            pairvec, pairvec, pairvec, pairvec, pairvec,
            full((D_MODEL, D_MODEL)), full((D_MODEL, D_MODEL)),
        ],
        out_specs=pl.BlockSpec((CHUNK, D_MODEL), real),
        out_shape=jax.ShapeDtypeStruct((batch * n_real * CHUNK, D_MODEL), F32),
        scratch_shapes=[
            pltpu.VMEM((CHUNK + 8, D_MODEL), F32), pltpu.VMEM((CHUNK + 8, D_MODEL), F32),
            pltpu.VMEM((CHUNK + 8, D_MODEL), F32), pltpu.VMEM((CHUNK + 8, SMALL_W), F32),
            pair_scr, pair_scr, pair_scr, pair_scr, pair_scr, pair_scr,
            pltpu.VMEM((RWKV_PAIRS, LANES, LANES), F32),
        ],
        compiler_params=pltpu.CompilerParams(
            dimension_semantics=("parallel", "arbitrary"), vmem_limit_bytes=VMEM_LIMIT),
        name="rwkv",
    )(p, p, p, p, p, mg, x2, *prm)


def _moe_body(h_ref, g_ref, wr_ref, br_ref, wgu_ref, bgu_ref, wdn_ref, bdn_ref, gf_ref,
              o_ref, u_scr, comb_scr, acc_scr):
    e = pl.program_id(1)
    lane = lax.broadcasted_iota(jnp.int32, (h_ref.shape[0], LANES), 1)

    @pl.when(e == 0)
    def _():
        h = h_ref[...]
        u = h * lax.rsqrt(jnp.mean(h * h, axis=-1, keepdims=True) + NORM_EPS) * g_ref[...]
        u_scr[...] = u.astype(BF16)
        hi, mid, lo = _split3(u)
        whi, wmid, wlo = wr_ref[0], wr_ref[1], wr_ref[2]
        logits = (_dot(hi, whi) + (_dot(hi, wmid) + _dot(mid, whi))
                  + (_dot(hi, wlo) + _dot(mid, wmid) + _dot(lo, whi))) + br_ref[...]
        logits = jnp.where(lane < N_EXPERTS, logits, -jnp.inf)
        comb = jnp.zeros_like(logits)
        denom = jnp.zeros((logits.shape[0], 1), F32)
        top = None
        for _ in range(TOP_K):
            m = jnp.max(logits, axis=-1, keepdims=True)
            idx = jnp.min(jnp.where(logits == m, lane, LANES), axis=-1, keepdims=True)
            hit = lane == idx
            top = m if top is None else top
            w = jnp.exp(m - top)
            comb = jnp.where(hit, w, comb)
            denom = denom + w
            logits = jnp.where(hit, -jnp.inf, logits)
        comb_scr[...] = comb / denom
        acc_scr[...] = jnp.zeros_like(acc_scr)

    hgu = _dot(u_scr[...], wgu_ref[0]) + bgu_ref[0]
    d = wdn_ref.shape[1]
    gate = jnp.minimum(hgu[:, :d], SWIGLU_LIMIT)
    up = jnp.clip(hgu[:, d:], -SWIGLU_LIMIT, SWIGLU_LIMIT)
    act = (up + 1.0) * (gate * jax.nn.sigmoid(gate * SWIGLU_ALPHA))
    out = _dot(act.astype(BF16), wdn_ref[0]) + bdn_ref[0]
    ce = jnp.sum(jnp.where(lane == e, comb_scr[...], 0.0), axis=-1, keepdims=True)
    acc_scr[...] += ce * out

    @pl.when(e == pl.num_programs(1) - 1)
    def _():
        h = h_ref[...] + acc_scr[...]
        o_ref[...] = h * lax.rsqrt(jnp.mean(h * h, axis=-1, keepdims=True) + NORM_EPS) * gf_ref[...]


def _moe(h2, g, wr3, br, wgu, bgu, wdn, bdn, gf, tm):
    t = h2.shape[0]
    return pl.pallas_call(
        _moe_body,
        grid=(t // tm, N_EXPERTS),
        in_specs=[
            pl.BlockSpec((tm, D_MODEL), lambda i, e: (i, 0)),
            pl.BlockSpec((1, D_MODEL), lambda i, e: (0, 0)),
            pl.BlockSpec((3, D_MODEL, LANES), lambda i, e: (0, 0, 0)),
            pl.BlockSpec((1, LANES), lambda i, e: (0, 0)),
            pl.BlockSpec((1, D_MODEL, 2 * D_MODEL), lambda i, e: (e, 0, 0)),
            pl.BlockSpec((1, 1, 2 * D_MODEL), lambda i, e: (e, 0, 0)),
            pl.BlockSpec((1, D_MODEL, D_MODEL), lambda i, e: (e, 0, 0)),
            pl.BlockSpec((1, 1, D_MODEL), lambda i, e: (e, 0, 0)),
            pl.BlockSpec((1, D_MODEL), lambda i, e: (0, 0)),
        ],
        out_specs=pl.BlockSpec((tm, D_MODEL), lambda i, e: (i, 0)),
        out_shape=jax.ShapeDtypeStruct((t, D_MODEL), F32),
        scratch_shapes=[pltpu.VMEM((tm, D_MODEL), BF16), pltpu.VMEM((tm, LANES), F32),
                        pltpu.VMEM((tm, D_MODEL), F32)],
        compiler_params=pltpu.CompilerParams(
            dimension_semantics=("parallel", "arbitrary"), vmem_limit_bytes=VMEM_LIMIT),
        name="moe",
    )(h2, g, wr3, br, wgu, bgu, wdn, bdn, gf)


def _pick_tile(n, prefs):
    for t in prefs:
        if n % t == 0:
            return t
    return n


def _pad_rows(w, rows, offset=0):
    out = jnp.zeros((rows, w.shape[1]), w.dtype)
    return out.at[offset:offset + w.shape[0]].set(w)


def kernel(x, meta_tokens, norm_mix_g, w_in, w_gla_a2, b_gla_a, gla_norm_g, w_gla_o, mu_r, mu_k, mu_v, mu_w, mu_a, mu_g, w_decay2, b_decay, w_a2, b_a, w_gate2, k_k, k_a, r_k, ln_x_g, ln_x_b, w_rwkv_o, w_out, norm_ffn_g, w_router, b_router, w_exp_gu, b_exp_gu, w_exp_down, b_exp_down, norm_final_g):
    batch, seq, d = x.shape
    assert d == D_MODEL and seq % CHUNK == 0 and w_in.shape[0] == 1
    lp = FRONT_PAD + N_META + seq
    n_chunks = lp // CHUNK
    row = lambda a: a.reshape(1, -1)

    meta = jnp.broadcast_to(meta_tokens[None].astype(x.dtype), (batch, N_META, d))
    hp = jnp.concatenate([jnp.zeros((batch, FRONT_PAD, d), x.dtype), meta, x], axis=1).reshape(batch * lp, d)

    splits = (GLA_KEY, GLA_KEY, D_MODEL, D_MODEL, GLA_GATE_RANK, D_MODEL, D_MODEL, D_MODEL,
              RWKV_DECAY_RANK, RWKV_A_RANK, RWKV_GATE_RANK, D_MODEL, D_MODEL)
    offs = [0]
    for s in splits:
        offs.append(offs[-1] + s)
    piece = lambda i: w_in[0][:, offs[i]:offs[i + 1]]
    zcols = lambda n: jnp.zeros((d, n), w_in.dtype)
    small = jnp.concatenate([piece(4), zcols(LANES - GLA_GATE_RANK), piece(8), piece(9), piece(10),
                             zcols(2 * LANES - RWKV_GATE_RANK)], axis=1)
    w_big = jnp.concatenate([piece(0), piece(1), piece(2), piece(3), piece(5), piece(6), piece(7),
                             piece(11), piece(12), small], axis=1).astype(BF16)

    m = batch * lp
    tm = _pick_tile(m, (640, 512, 320, 256, 192, 128, 64))
    tn = _pick_tile(NP_COLS, (2176, 512))
    p = _inproj(hp, row(norm_mix_g[0]), w_big, tm, tn)

    mg = _gla(p, _pad_rows(w_gla_a2[0], LANES).astype(BF16), row(b_gla_a[0]), row(gla_norm_g[0]),
              w_gla_o[0].astype(BF16), batch, n_chunks)

    pv = lambda a: a.reshape(RWKV_PAIRS, 1, LANES)
    mu_small = jnp.concatenate([jnp.zeros((LANES,), F32), mu_w[0], mu_a[0], mu_g[0],
                                jnp.zeros((2 * LANES - RWKV_GATE_RANK,), F32)])
    prm = (row(mu_r[0]), row(mu_k[0]), row(mu_v[0]), row(mu_small),
           _pad_rows(w_decay2[0], LANES).astype(BF16), row(b_decay[0]),
           _pad_rows(w_a2[0], LANES, RWKV_DECAY_RANK).astype(BF16), row(b_a[0]),
           _pad_rows(w_gate2[0], 2 * LANES).astype(BF16),
           pv(k_k[0]), pv(k_a[0]), pv(r_k[0]), pv(ln_x_g[0]), pv(ln_x_b[0]),
           w_rwkv_o[0].astype(BF16), w_out[0].astype(BF16))
    h2 = _rwkv(p, mg, x.reshape(batch * seq, d), prm, batch, n_chunks)

    wr = jnp.zeros((d, LANES), F32).at[:, :N_EXPERTS].set(w_router[0])
    wr3 = jnp.stack(_split3(wr))
    br = jnp.zeros((1, LANES), F32).at[0, :N_EXPERTS].set(b_router[0])
    t = batch * seq
    out = _moe(h2, row(norm_ffn_g[0]), wr3, br, w_exp_gu[0].astype(BF16), b_exp_gu[0][:, None, :],
               w_exp_down[0].astype(BF16), b_exp_down[0][:, None, :], row(norm_final_g),
               _pick_tile(t, (512, 256, 128)))
    return out.reshape(batch, seq, d)
```

```python
import functools

import jax
import jax.numpy as jnp
from jax import lax
from jax.experimental import pallas as pl
from jax.experimental.pallas import tpu as pltpu

F32 = jnp.float32
BF16 = jnp.bfloat16

D_MODEL = 1024
N_META = 16
NORM_EPS = 1e-5
CHUNK = 64
FRONT_PAD = (-N_META) % CHUNK
GLA_HEADS = 4
GLA_DK = 128
GLA_DV = 256
GLA_KEY = GLA_HEADS * GLA_DK
GLA_GATE_RANK = 16
GLA_GATE_NORM = 16.0
RWKV_HEAD = 64
RWKV_PAIRS = D_MODEL // (2 * RWKV_HEAD)
RWKV_DECAY_RANK = 64
RWKV_A_RANK = 64
RWKV_GATE_RANK = 160
RWKV_GN_EPS = 64e-5
N_EXPERTS = 32
TOP_K = 4
SWIGLU_LIMIT = 7.0
SWIGLU_ALPHA = 1.702
LANES = 128
SMALL_W = 512
COL_GQ, COL_GK, COL_GV, COL_GR = 0, 512, 1024, 2048
COL_RR, COL_RK, COL_RV = 3072, 4096, 5120
COL_GATE_GLA, COL_GATE_RWKV, COL_SMALL = 6144, 7168, 8192
NP_COLS = COL_SMALL + SMALL_W
VMEM_LIMIT = 56 * 1024 * 1024


def _dot(a, b):
    return jnp.dot(a, b, preferred_element_type=F32)


def _dot_nt(a, b):
    return lax.dot_general(a, b, (((1,), (1,)), ((), ())), preferred_element_type=F32)


def _dot_tn(a, b):
    return lax.dot_general(a, b, (((0,), (0,)), ((), ())), preferred_element_type=F32)


def _split2(x):
    hi = x.astype(BF16)
    lo = (x - hi.astype(F32)).astype(BF16)
    return hi, lo


def _split3(x):
    hi = x.astype(BF16)
    r1 = x - hi.astype(F32)
    mid = r1.astype(BF16)
    lo = (r1 - mid.astype(F32)).astype(BF16)
    return hi, mid, lo


def _cumsum_rows(x, tri):
    hi, mid, lo = _split3(x)
    return _dot(tri, hi) + _dot(tri, mid) + _dot(tri, lo)


def _softplus(x):
    return jnp.maximum(x, 0.0) + jnp.log1p(jnp.exp(-jnp.abs(x)))


def _tri_incl(n):
    r = lax.broadcasted_iota(jnp.int32, (n, n), 0)
    c = lax.broadcasted_iota(jnp.int32, (n, n), 1)
    return (r >= c).astype(BF16)


def _inproj_body(x_ref, g_ref, w_ref, o_ref, u_scr):
    @pl.when(pl.program_id(1) == 0)
    def _():
        x = x_ref[...]
        ms = jnp.mean(x * x, axis=-1, keepdims=True)
        u_scr[...] = (x * lax.rsqrt(ms + NORM_EPS) * g_ref[...]).astype(BF16)

    o_ref[...] = _dot(u_scr[...], w_ref[...])


def _inproj(hp, g, w, tm, tn):
    m = hp.shape[0]
    return pl.pallas_call(
        _inproj_body,
        grid=(m // tm, NP_COLS // tn),
        in_specs=[
            pl.BlockSpec((tm, D_MODEL), lambda i, j: (i, 0)),
            pl.BlockSpec((1, D_MODEL), lambda i, j: (0, 0)),
            pl.BlockSpec((D_MODEL, tn), lambda i, j: (0, j)),
        ],
        out_specs=pl.BlockSpec((tm, tn), lambda i, j: (i, j)),
        out_shape=jax.ShapeDtypeStruct((m, NP_COLS), F32),
        scratch_shapes=[pltpu.VMEM((tm, D_MODEL), BF16)],
        compiler_params=pltpu.CompilerParams(
            dimension_semantics=("parallel", "arbitrary"), vmem_limit_bytes=VMEM_LIMIT),
        name="inproj",
    )(hp, g, w)


def _gla_body(q_ref, k_ref, v_ref, r_ref, sm_ref, gate_ref, wa2_ref, ba_ref, ng_ref, wo_ref,
              o_ref, s_scr):
    c = pl.program_id(1)

    @pl.when(c == 0)
    def _():
        s_scr[...] = jnp.zeros_like(s_scr)

    tri = _tri_incl(CHUNK)
    row = lax.broadcasted_iota(jnp.int32, (CHUNK, 1), 0)
    col = lax.broadcasted_iota(jnp.int32, (CHUNK, CHUNK), 1)
    causal = lax.broadcasted_iota(jnp.int32, (CHUNK, CHUNK), 0) >= col

    z = _dot(sm_ref[:, 0:LANES].astype(BF16), wa2_ref[...]) + ba_ref[...]
    gk = -_softplus(-z) * (1.0 / GLA_GATE_NORM)
    gk = jnp.where((c > 0) | (row >= FRONT_PAD), gk, 0.0)
    b = _cumsum_rows(gk, tri)
    b_ref = b[CHUNK // 2 - 1:CHUNK // 2, :]
    b_last = b[CHUNK - 1:CHUNK, :]

    q = q_ref[...] * (GLA_DK ** -0.5)
    k = k_ref[...]
    qe = (q * jnp.exp(b - b_ref)).astype(BF16)
    ke = (k * jnp.exp(b_ref - b)).astype(BF16)
    qs = (q * jnp.exp(b)).astype(BF16)
    kd = (k * jnp.exp(b_last - b)).astype(BF16)
    decay = jnp.exp(b_last)
    v = v_ref[...].astype(BF16)
    silu_r = r_ref[...]
    silu_r = silu_r * jax.nn.sigmoid(silu_r)

    outs = []
    for h in range(GLA_HEADS):
        ks = slice(h * GLA_DK, (h + 1) * GLA_DK)
        vs = slice(h * GLA_DV, (h + 1) * GLA_DV)
        a = jnp.where(causal, _dot_nt(qe[:, ks], ke[:, ks]), 0.0)
        st = s_scr[h]
        o = _dot(a.astype(BF16), v[:, vs]) + _dot_nt(qs[:, ks], st.astype(BF16))
        s_scr[h] = st * decay[:, ks] + _dot_tn(v[:, vs], kd[:, ks])
        o = o * lax.rsqrt(jnp.mean(o * o, axis=-1, keepdims=True) + NORM_EPS) * ng_ref[...]
        outs.append(o * silu_r[:, vs])
    og = jnp.concatenate(outs, axis=1).astype(BF16)
    o_ref[...] = jax.nn.sigmoid(gate_ref[...]) * _dot(og, wo_ref[...])


def _gla(p, wa2, ba, ng, wo, batch, n_chunks):
    rows = lambda b, c: b * n_chunks + c
    pspec = lambda w, col: pl.BlockSpec((CHUNK, w), lambda b, c: (rows(b, c), col // w))
    full = lambda shape: pl.BlockSpec(shape, lambda b, c: (0,) * len(shape))
    n_real = n_chunks - 1
    return pl.pallas_call(
        _gla_body,
        grid=(batch, n_chunks),
        in_specs=[
            pspec(GLA_KEY, COL_GQ), pspec(GLA_KEY, COL_GK), pspec(D_MODEL, COL_GV),
            pspec(D_MODEL, COL_GR), pspec(SMALL_W, COL_SMALL), pspec(D_MODEL, COL_GATE_GLA),
            full((LANES, GLA_KEY)), full((1, GLA_KEY)), full((1, GLA_DV)), full((D_MODEL, D_MODEL)),
        ],
        out_specs=pl.BlockSpec((CHUNK, D_MODEL), lambda b, c: (b * n_real + jnp.maximum(c - 1, 0), 0)),
        out_shape=jax.ShapeDtypeStruct((batch * n_real * CHUNK, D_MODEL), F32),
        scratch_shapes=[pltpu.VMEM((GLA_HEADS, GLA_DV, GLA_DK), F32)],
        compiler_params=pltpu.CompilerParams(
            dimension_semantics=("parallel", "arbitrary"), vmem_limit_bytes=VMEM_LIMIT),
        name="gla",
    )(p, p, p, p, p, p, wa2, ba, ng, wo)


def _stack_heads(x, lane_lo):
    return jnp.concatenate([jnp.where(lane_lo, x, 0.0), jnp.where(lane_lo, 0.0, x)], axis=1)


def _bmm(a, b):
    return jnp.einsum("gik,gkj->gij", a, b, preferred_element_type=F32)


def _bmm_nt(a, b):
    return jnp.einsum("gik,gjk->gij", a, b, preferred_element_type=F32)


def _bmm_tn(a, b):
    return jnp.einsum("gti,gtj->gij", a, b, preferred_element_type=F32)


def _rwkv_body(pr_ref, pk_ref, pv_ref, sm_ref, gate_ref, mg_ref, x_ref,
               mur_ref, muk_ref, muv_ref, musm_ref, wd_ref, bd_ref, wa_ref, ba_ref, wg_ref,
               kk_ref, ka_ref, rk_ref, lng_ref, lnb_ref, wo_ref, wout_ref,
               o_ref,
               shr_scr, shk_scr, shv_scr, shs_scr, s_scr):
    c = pl.program_id(1)
    C = CHUNK
    G = RWKV_PAIRS

    @pl.when(c == 0)
    def _():
        s_scr[...] = jnp.zeros_like(s_scr)
        for scr in (shr_scr, shk_scr, shv_scr, shs_scr):
            scr[7:8, :] = jnp.zeros((1, scr.shape[1]), F32)

    def lerp(x_ref_, scr, mu_ref):
        x = x_ref_[...]
        scr[8:8 + C, :] = x
        prev = scr[7:7 + C, :]
        scr[7:8, :] = x[C - 1:C, :]
        return x + (prev - x) * mu_ref[...]

    def pairs(x):
        return jnp.stack([x[:, p * LANES:(p + 1) * LANES] for p in range(G)], axis=0)

    r = lerp(pr_ref, shr_scr, mur_ref)
    k = lerp(pk_ref, shk_scr, muk_ref)
    v = lerp(pv_ref, shv_scr, muv_ref)
    ls = lerp(sm_ref, shs_scr, musm_ref)
    s1 = ls[:, LANES:2 * LANES]
    xw = bd_ref[...] + _dot(jnp.tanh(s1).astype(BF16), wd_ref[...])
    logw = -jnp.exp(-_softplus(-xw) - 0.5)
    a = jax.nn.sigmoid(ba_ref[...] + _dot(s1.astype(BF16), wa_ref[...]))
    g = _dot(jax.nn.sigmoid(ls[:, 2 * LANES:4 * LANES]).astype(BF16), wg_ref[...])

    cl = _cumsum_rows(logw, _tri_incl(C))
    cref = cl[C // 2 - 1:C // 2, :]
    clast = cl[C - 1:C, :]
    e_neg = pairs(jnp.exp(cref - cl))
    e_prev = pairs(jnp.exp(cl - logw - cref))
    e_cur = pairs(jnp.exp(cl - cref))
    g_last = pairs(jnp.exp(clast - cref))
    g_ref = pairs(jnp.exp(cref))
    g_c = pairs(jnp.exp(clast))
    r = pairs(r)
    k = pairs(k)
    v = pairs(v)
    a = pairs(a)

    lane_lo = lax.broadcasted_iota(jnp.int32, (1, C, LANES), 2) < RWKV_HEAD
    er = lax.broadcasted_iota(jnp.int32, (LANES, LANES), 0)
    ec = lax.broadcasted_iota(jnp.int32, (LANES, LANES), 1)
    seg_ones = ((er // RWKV_HEAD) == (ec // RWKV_HEAD)).astype(BF16)
    strict = ((er % C) > (ec % C))[None]
    incl = ((er % C) >= (ec % C))[None]

    def seg(x):
        hi, lo = _split2(x.reshape(G * C, LANES))
        return (_dot(hi, seg_ones) + _dot(lo, seg_ones)).reshape(G, C, LANES)

    kk = k * kk_ref[...]
    kk = kk / jnp.maximum(jnp.sqrt(seg(kk * kk)), 1e-12)
    k2 = k * (1.0 + (a - 1.0) * ka_ref[...])
    bonus = seg(r * k2 * rk_ref[...])
    at = -kk * e_prev
    rt = r * e_cur
    bt = kk * a * e_neg
    kt = k2 * e_neg
    st = s_scr[...]
    sp = (st * g_ref).astype(BF16)

    lhs = jnp.concatenate([_stack_heads(at, lane_lo), _stack_heads(rt, lane_lo)], axis=1).astype(BF16)
    rhs = jnp.concatenate([_stack_heads(bt, lane_lo), _stack_heads(kt, lane_lo)], axis=1).astype(BF16)
    m1 = _bmm_nt(lhs, rhs)
    n2 = 2 * C
    ab = jnp.where(strict, m1[:, :n2, :n2], 0.0).astype(BF16)
    ak = jnp.where(strict, m1[:, :n2, n2:], 0.0).astype(BF16)
    rb = jnp.where(incl, m1[:, n2:, :n2], 0.0).astype(BF16)
    rk = jnp.where(incl, m1[:, n2:, n2:], 0.0).astype(BF16)
    vs = _stack_heads(v, lane_lo).astype(BF16)

    pm = _bmm_nt(lhs[:, :n2], sp) + _bmm(ak, vs)
    x = ab
    for i in range(6):
        pm = pm + _bmm(x, pm.astype(BF16))
        if i < 5:
            x = _bmm(x, x).astype(BF16)
    pmb = pm.astype(BF16)
    y2 = _bmm_nt(lhs[:, n2:], sp) + _bmm(rb, pmb) + _bmm(rk, vs)
    y = y2[:, :C] + y2[:, C:]

    upd_l = jnp.concatenate([pmb, vs], axis=1)
    upd_r = jnp.concatenate([_stack_heads(bt * g_last, lane_lo),
                             _stack_heads(kt * g_last, lane_lo)], axis=1).astype(BF16)
    s_scr[...] = st * g_c + _bmm_tn(upd_l, upd_r)

    mean = seg(y) * (1.0 / RWKV_HEAD)
    yc = y - mean
    var = seg(yc * yc) * (1.0 / RWKV_HEAD)
    y = yc * lax.rsqrt(var + RWKV_GN_EPS) * lng_ref[...] + lnb_ref[...] + bonus * v

    y = jnp.concatenate([y[p] for p in range(G)], axis=1)
    orw = _dot((y * g).astype(BF16), wo_ref[...])
    mix = mg_ref[...] + jax.nn.sigmoid(gate_ref[...]) * orw
    o_ref[...] = x_ref[...] + _dot(mix.astype(BF16), wout_ref[...])


def _rwkv(p, mg, x2, prm, batch, n_chunks):
    rows = lambda b, c: b * n_chunks + c
    n_real = n_chunks - 1
    real = lambda b, c: (b * n_real + jnp.maximum(c - 1, 0), 0)
    pspec = lambda w, col: pl.BlockSpec((CHUNK, w), lambda b, c: (rows(b, c), col // w))
    full = lambda shape: pl.BlockSpec(shape, lambda b, c: (0,) * len(shape))
    pairvec = full((RWKV_PAIRS, 1, LANES))
    vec = full((1, D_MODEL))
    return pl.pallas_call(
        _rwkv_body,
        grid=(batch, n_chunks),
        in_specs=[
            pspec(D_MODEL, COL_RR), pspec(D_MODEL, COL_RK), pspec(D_MODEL, COL_RV),
            pspec(SMALL_W, COL_SMALL), pspec(D_MODEL, COL_GATE_RWKV),
            pl.BlockSpec((CHUNK, D_MODEL), real), pl.BlockSpec((CHUNK, D_MODEL), real),
            vec, vec, vec, full((1, SMALL_W)),
            full((LANES, D_MODEL)), vec, full((LANES, D_MODEL)), vec, full((2 * LANES, D_MODEL)),
            pairvec, pairvec, pairvec, pairvec, pairvec,
            full((D_MODEL, D_MODEL)), full((D_MODEL, D_MODEL)),
        ],
        out_specs=pl.BlockSpec((CHUNK, D_MODEL), real),
        out_shape=jax.ShapeDtypeStruct((batch * n_real * CHUNK, D_MODEL), F32),
        scratch_shapes=[
            pltpu.VMEM((CHUNK + 8, D_MODEL), F32), pltpu.VMEM((CHUNK + 8, D_MODEL), F32),
            pltpu.VMEM((CHUNK + 8, D_MODEL), F32), pltpu.VMEM((CHUNK + 8, SMALL_W), F32),
            pltpu.VMEM((RWKV_PAIRS, LANES, LANES), F32),
        ],
        compiler_params=pltpu.CompilerParams(
            dimension_semantics=("parallel", "arbitrary"), vmem_limit_bytes=VMEM_LIMIT),
        name="rwkv",
    )(p, p, p, p, p, mg, x2, *prm)


def _route_body(h_ref, g_ref, wr_ref, br_ref, idx_ref, prob_ref, rank_ref, cnt_ref, base_scr):
    tm = h_ref.shape[0]

    @pl.when(pl.program_id(0) == 0)
    def _():
        base_scr[...] = jnp.zeros_like(base_scr)

    h = h_ref[...]
    u = h * lax.rsqrt(jnp.mean(h * h, axis=-1, keepdims=True) + NORM_EPS) * g_ref[...]
    hi, mid, lo = _split3(u)
    whi, wmid, wlo = wr_ref[0], wr_ref[1], wr_ref[2]
    logits = (_dot(hi, whi) + (_dot(hi, wmid) + _dot(mid, whi))
              + (_dot(hi, wlo) + _dot(mid, wmid) + _dot(lo, whi))) + br_ref[...]
    lane = lax.broadcasted_iota(jnp.int32, (tm, LANES), 1)
    logits = jnp.where(lane < N_EXPERTS, logits, -jnp.inf)
    idx_out = jnp.zeros((tm, LANES), jnp.int32)
    prob_out = jnp.zeros((tm, LANES), F32)
    denom = jnp.zeros((tm, 1), F32)
    hits = []
    top = None
    for k in range(TOP_K):
        m = jnp.max(logits, axis=-1, keepdims=True)
        idx = jnp.min(jnp.where(logits == m, lane, LANES), axis=-1, keepdims=True)
        hit = lane == idx
        top = m if top is None else top
        w = jnp.exp(m - top)
        idx_out = jnp.where(lane == k, idx, idx_out)
        prob_out = jnp.where(lane == k, w, prob_out)
        denom = denom + w
        hits.append(hit)
        logits = jnp.where(hit, -jnp.inf, logits)
    idx_ref[...] = idx_out
    prob_ref[...] = prob_out / denom

    tot = jnp.zeros((tm, LANES), F32)
    for hit in hits:
        tot = tot + jnp.where(hit, 1.0, 0.0)
    r = lax.broadcasted_iota(jnp.int32, (tm, tm), 0)
    c = lax.broadcasted_iota(jnp.int32, (tm, tm), 1)
    before = _dot((r > c).astype(BF16), tot.astype(BF16)) + base_scr[...]
    rank_out = jnp.zeros((tm, LANES), F32)
    for k, hit in enumerate(hits):
        rk = jnp.sum(jnp.where(hit, before, 0.0), axis=-1, keepdims=True)
        rank_out = jnp.where(lane == k, rk, rank_out)
    rank_ref[...] = rank_out.astype(jnp.int32)
    base_scr[...] += jnp.sum(tot, axis=0, keepdims=True)
    cnt_ref[...] = base_scr[...]


def _route(h2, g, wr3, br, tm):
    t = h2.shape[0]
    full = lambda shape: pl.BlockSpec(shape, lambda i: (0,) * len(shape))
    tile = lambda w: pl.BlockSpec((tm, w), lambda i: (i, 0))
    return pl.pallas_call(
        _route_body,
        grid=(t // tm,),
        in_specs=[tile(D_MODEL), full((1, D_MODEL)), full((3, D_MODEL, LANES)), full((1, LANES))],
        out_specs=[tile(LANES), tile(LANES), tile(LANES), full((1, LANES))],
        out_shape=[jax.ShapeDtypeStruct((t, LANES), jnp.int32),
                   jax.ShapeDtypeStruct((t, LANES), F32),
                   jax.ShapeDtypeStruct((t, LANES), jnp.int32),
                   jax.ShapeDtypeStruct((1, LANES), F32)],
        scratch_shapes=[pltpu.VMEM((1, LANES), F32)],
        compiler_params=pltpu.CompilerParams(
            dimension_semantics=("arbitrary",), vmem_limit_bytes=VMEM_LIMIT),
        name="route",
    )(h2, g, wr3, br)


def _dispatch_body(dest_ref, h_ref, xs_in_ref, xs_ref, sem):
    del xs_in_ref
    tm = h_ref.shape[0]

    def issue(t, carry):
        for k in range(TOP_K):
            d = dest_ref[0, t * TOP_K + k]
            pltpu.make_async_copy(h_ref.at[pl.ds(t, 1)], xs_ref.at[pl.ds(d, 1)], sem).start()
        return carry

    lax.fori_loop(0, tm, issue, 0)
    for k in range(TOP_K):
        pltpu.make_async_copy(h_ref, xs_ref.at[pl.ds(0, tm)], sem).wait()


def _dispatch(dest2, h2, xs0, tm):
    t, w = h2.shape
    return pl.pallas_call(
        _dispatch_body,
        grid=(t // tm,),
        in_specs=[pl.BlockSpec((None, 1, tm * TOP_K), lambda i: (i, 0, 0), memory_space=pltpu.SMEM),
                  pl.BlockSpec((tm, w), lambda i: (i, 0)),
                  pl.BlockSpec(memory_space=pl.ANY)],
        out_specs=pl.BlockSpec(memory_space=pl.ANY),
        out_shape=jax.ShapeDtypeStruct(xs0.shape, xs0.dtype),
        scratch_shapes=[pltpu.SemaphoreType.DMA(())],
        input_output_aliases={2: 0},
        compiler_params=pltpu.CompilerParams(
            dimension_semantics=("arbitrary",), vmem_limit_bytes=VMEM_LIMIT),
        name="dispatch",
    )(dest2, h2, xs0)


def _experts_body(te_ref, nu_ref, xs_ref, g_ref, wgu_ref, bgu_ref, wdn_ref, bdn_ref, ys_ref):
    del te_ref
    i = pl.program_id(0)

    @pl.when(i < nu_ref[0])
    def _():
        x = xs_ref[...]
        u = x * lax.rsqrt(jnp.mean(x * x, axis=-1, keepdims=True) + NORM_EPS) * g_ref[...]
        hgu = _dot(u.astype(BF16), wgu_ref[0]) + bgu_ref[0]
        d = wdn_ref.shape[1]
        gate = jnp.minimum(hgu[:, :d], SWIGLU_LIMIT)
        up = jnp.clip(hgu[:, d:], -SWIGLU_LIMIT, SWIGLU_LIMIT)
        act = (up + 1.0) * (gate * jax.nn.sigmoid(gate * SWIGLU_ALPHA))
        ys_ref[...] = _dot(act.astype(BF16), wdn_ref[0]) + bdn_ref[0]

    @pl.when(i >= nu_ref[0])
    def _():
        ys_ref[...] = jnp.zeros_like(ys_ref)


def _experts(tile_expert, n_used, xs, g, wgu, bgu, wdn, bdn, tm):
    ns = xs.shape[0]
    wsel = lambda shape: pl.BlockSpec(shape, lambda i, te, nu: (te[i], 0, 0))
    return pl.pallas_call(
        _experts_body,
        grid_spec=pltpu.PrefetchScalarGridSpec(
            num_scalar_prefetch=2,
            grid=(ns // tm,),
            in_specs=[pl.BlockSpec((tm, D_MODEL), lambda i, te, nu: (i, 0)),
                      pl.BlockSpec((1, D_MODEL), lambda i, te, nu: (0, 0)),
                      wsel((1, D_MODEL, 2 * D_MODEL)), wsel((1, 1, 2 * D_MODEL)),
                      wsel((1, D_MODEL, D_MODEL)), wsel((1, 1, D_MODEL))],
            out_specs=pl.BlockSpec((tm, D_MODEL), lambda i, te, nu: (i, 0)),
        ),
        out_shape=jax.ShapeDtypeStruct((ns, D_MODEL), F32),
        compiler_params=pltpu.CompilerParams(
            dimension_semantics=("arbitrary",), vmem_limit_bytes=VMEM_LIMIT),
        name="experts",
    )(tile_expert, n_used, xs, g, wgu, bgu, wdn, bdn)


def _combine_body(dcur_ref, dnext_ref, prob_ref, h_ref, gf_ref, ys_ref, o_ref, ybuf, sem):
    i = pl.program_id(0)
    n = pl.num_programs(0)
    tm = h_ref.shape[0]
    slot = i % 2

    def gather(dref, s):
        def issue(t, carry):
            for k in range(TOP_K):
                d = dref[0, t * TOP_K + k]
                pltpu.make_async_copy(ys_ref.at[pl.ds(d, 1)], ybuf.at[s, k, pl.ds(t, 1)], sem.at[s]).start()
            return carry
        lax.fori_loop(0, tm, issue, 0)

    @pl.when(i == 0)
    def _():
        gather(dcur_ref, 0)

    @pl.when(i + 1 < n)
    def _():
        gather(dnext_ref, 1 - slot)

    for k in range(TOP_K):
        pltpu.make_async_copy(ys_ref.at[pl.ds(0, tm)], ybuf.at[slot, k], sem.at[slot]).wait()
    prob = prob_ref[...]
    h = h_ref[...]
    for k in range(TOP_K):
        h = h + prob[:, k:k + 1] * ybuf[slot, k]
    o_ref[...] = h * lax.rsqrt(jnp.mean(h * h, axis=-1, keepdims=True) + NORM_EPS) * gf_ref[...]


def _combine(dest2, prob, h2, gf, ys, tm):
    t = h2.shape[0]
    n = t // tm
    smem = lambda imap: pl.BlockSpec((None, 1, tm * TOP_K), imap, memory_space=pltpu.SMEM)
    return pl.pallas_call(
        _combine_body,
        grid=(n,),
        in_specs=[smem(lambda i: (i, 0, 0)), smem(lambda i: (jnp.minimum(i + 1, n - 1), 0, 0)),
                  pl.BlockSpec((tm, LANES), lambda i: (i, 0)),
                  pl.BlockSpec((tm, D_MODEL), lambda i: (i, 0)),
                  pl.BlockSpec((1, D_MODEL), lambda i: (0, 0)),
                  pl.BlockSpec(memory_space=pl.ANY)],
        out_specs=pl.BlockSpec((tm, D_MODEL), lambda i: (i, 0)),
        out_shape=jax.ShapeDtypeStruct((t, D_MODEL), F32),
        scratch_shapes=[pltpu.VMEM((2, TOP_K, tm, D_MODEL), F32), pltpu.SemaphoreType.DMA((2,))],
        compiler_params=pltpu.CompilerParams(
            dimension_semantics=("arbitrary",), vmem_limit_bytes=VMEM_LIMIT),
        name="combine",
    )(dest2, dest2, prob, h2, gf, ys)


def _moe(h2, g, wr3, br, wgu, bgu, wdn, bdn, gf):
    t = h2.shape[0]
    tm_route = _pick_tile(t, (512, 256, 128))
    tm_x = _pick_tile(t, (512, 256, 128))
    tm_c = _pick_tile(t, (256, 128))
    idx, prob, rank, cnt = _route(h2, g, wr3, br, tm_route)

    counts = cnt[0, :N_EXPERTS].astype(jnp.int32)
    padded = (counts + tm_x - 1) // tm_x * tm_x
    ends = jnp.cumsum(padded)
    offs = ends - padded
    ns = t * TOP_K + N_EXPERTS * tm_x
    e_ids = jnp.arange(N_EXPERTS, dtype=jnp.int32)
    idx4 = idx[:, :TOP_K]
    dest = rank[:, :TOP_K] + jnp.sum(jnp.where(idx4[..., None] == e_ids, offs, 0), axis=-1)
    tile_start = jnp.arange(ns // tm_x, dtype=jnp.int32) * tm_x
    tile_expert = jnp.minimum(jnp.sum((tile_start[:, None] >= ends[None, :]).astype(jnp.int32), axis=1),
                              N_EXPERTS - 1)
    n_used = (ends[-1] // tm_x).reshape(1)

    xs = _dispatch(dest.reshape(t // tm_route, 1, tm_route * TOP_K), h2,
                   jnp.zeros((ns, D_MODEL), F32), tm_route)
    ys = _experts(tile_expert, n_used, xs, g, wgu, bgu, wdn, bdn, tm_x)
    return _combine(dest.reshape(t // tm_c, 1, tm_c * TOP_K), prob, h2, gf, ys, tm_c)


def _pick_tile(n, prefs):
    for t in prefs:
        if n % t == 0:
            return t
    return n


def _pad_rows(w, rows, offset=0):
    out = jnp.zeros((rows, w.shape[1]), w.dtype)
    return out.at[offset:offset + w.shape[0]].set(w)


def kernel(x, meta_tokens, norm_mix_g, w_in, w_gla_a2, b_gla_a, gla_norm_g, w_gla_o, mu_r, mu_k, mu_v, mu_w, mu_a, mu_g, w_decay2, b_decay, w_a2, b_a, w_gate2, k_k, k_a, r_k, ln_x_g, ln_x_b, w_rwkv_o, w_out, norm_ffn_g, w_router, b_router, w_exp_gu, b_exp_gu, w_exp_down, b_exp_down, norm_final_g):
    batch, seq, d = x.shape
    assert d == D_MODEL and seq % CHUNK == 0 and w_in.shape[0] == 1
    lp = FRONT_PAD + N_META + seq
    n_chunks = lp // CHUNK
    row = lambda a: a.reshape(1, -1)

    meta = jnp.broadcast_to(meta_tokens[None].astype(x.dtype), (batch, N_META, d))
    hp = jnp.concatenate([jnp.zeros((batch, FRONT_PAD, d), x.dtype), meta, x], axis=1).reshape(batch * lp, d)

    splits = (GLA_KEY, GLA_KEY, D_MODEL, D_MODEL, GLA_GATE_RANK, D_MODEL, D_MODEL, D_MODEL,
              RWKV_DECAY_RANK, RWKV_A_RANK, RWKV_GATE_RANK, D_MODEL, D_MODEL)
    offs = [0]
    for s in splits:
        offs.append(offs[-1] + s)
    piece = lambda i: w_in[0][:, offs[i]:offs[i + 1]]
    zcols = lambda n: jnp.zeros((d, n), w_in.dtype)
    small = jnp.concatenate([piece(4), zcols(LANES - GLA_GATE_RANK), piece(8), piece(9), piece(10),
                             zcols(2 * LANES - RWKV_GATE_RANK)], axis=1)
    w_big = jnp.concatenate([piece(0), piece(1), piece(2), piece(3), piece(5), piece(6), piece(7),
                             piece(11), piece(12), small], axis=1).astype(BF16)

    m = batch * lp
    tm = _pick_tile(m, (640, 512, 320, 256, 192, 128, 64))
    tn = _pick_tile(NP_COLS, (2176, 512))
    p = _inproj(hp, row(norm_mix_g[0]), w_big, tm, tn)

    mg = _gla(p, _pad_rows(w_gla_a2[0], LANES).astype(BF16), row(b_gla_a[0]), row(gla_norm_g[0]),
              w_gla_o[0].astype(BF16), batch, n_chunks)

    pv = lambda a: a.reshape(RWKV_PAIRS, 1, LANES)
    mu_small = jnp.concatenate([jnp.zeros((LANES,), F32), mu_w[0], mu_a[0], mu_g[0],
                                jnp.zeros((2 * LANES - RWKV_GATE_RANK,), F32)])
    prm = (row(mu_r[0]), row(mu_k[0]), row(mu_v[0]), row(mu_small),
           _pad_rows(w_decay2[0], LANES).astype(BF16), row(b_decay[0]),
           _pad_rows(w_a2[0], LANES, RWKV_DECAY_RANK).astype(BF16), row(b_a[0]),
           _pad_rows(w_gate2[0], 2 * LANES).astype(BF16),
           pv(k_k[0]), pv(k_a[0]), pv(r_k[0]), pv(ln_x_g[0]), pv(ln_x_b[0]),
           w_rwkv_o[0].astype(BF16), w_out[0].astype(BF16))
    h2 = _rwkv(p, mg, x.reshape(batch * seq, d), prm, batch, n_chunks)

    wr = jnp.zeros((d, LANES), F32).at[:, :N_EXPERTS].set(w_router[0])
    wr3 = jnp.stack(_split3(wr))
    br = jnp.zeros((1, LANES), F32).at[0, :N_EXPERTS].set(b_router[0])
    out = _moe(h2, row(norm_ffn_g[0]), wr3, br, w_exp_gu[0].astype(BF16), b_exp_gu[0][:, None, :],
               w_exp_down[0].astype(BF16), b_exp_down[0][:, None, :], row(norm_final_g))
    return out.reshape(batch, seq, d)
```

```python
import jax
import jax.numpy as jnp
from jax import lax
from jax.experimental import pallas as pl
from jax.experimental.pallas import tpu as pltpu

F32 = jnp.float32
BF16 = jnp.bfloat16

D_MODEL = 1024
N_META = 16
NORM_EPS = 1e-5
CHUNK = 64
FRONT_PAD = (-N_META) % CHUNK
GLA_HEADS = 4
GLA_DK = 128
GLA_DV = 256
GLA_KEY = GLA_HEADS * GLA_DK
GLA_GATE_RANK = 16
GLA_GATE_NORM = 16.0
RWKV_HEAD = 64
RWKV_PAIRS = D_MODEL // (2 * RWKV_HEAD)
RWKV_DECAY_RANK = 64
RWKV_A_RANK = 64
RWKV_GATE_RANK = 160
RWKV_GN_EPS = 64e-5
N_EXPERTS = 32
TOP_K = 4
SWIGLU_LIMIT = 7.0
SWIGLU_ALPHA = 1.702
SEQS_PER_STEP = 2
DMA_UNROLL = 8
LANES = 128
SUBLANES = 8
SMALL_W = 512
COL_GQ, COL_GK, COL_GV, COL_GR = 0, 512, 1024, 2048
COL_RR, COL_RK, COL_RV = 3072, 4096, 5120
COL_GATE_GLA, COL_GATE_RWKV, COL_SMALL = 6144, 7168, 8192
NP_COLS = COL_SMALL + SMALL_W
VMEM_LIMIT = 56 * 1024 * 1024


def _dot(a, b):
    return jnp.dot(a, b, preferred_element_type=F32)


def _bmm(a, b):
    return jnp.einsum("gik,gkj->gij", a, b, preferred_element_type=F32)


def _bmm_nt(a, b):
    return jnp.einsum("gik,gjk->gij", a, b, preferred_element_type=F32)


def _bmm_tn(a, b):
    return jnp.einsum("gti,gtj->gij", a, b, preferred_element_type=F32)


def _split2(x):
    hi = x.astype(BF16)
    lo = (x - hi.astype(F32)).astype(BF16)
    return hi, lo


def _split3(x):
    hi = x.astype(BF16)
    r1 = x - hi.astype(F32)
    mid = r1.astype(BF16)
    lo = (r1 - mid.astype(F32)).astype(BF16)
    return hi, mid, lo


def _cumsum_rows(x, tri):
    hi, mid, lo = _split3(x)
    return _dot(tri, hi) + _dot(tri, mid) + _dot(tri, lo)


def _softplus(x):
    return jnp.maximum(x, 0.0) + jnp.log1p(jnp.exp(-jnp.abs(x)))


def _tri_blocks(rows, blk):
    r = lax.broadcasted_iota(jnp.int32, (rows, rows), 0)
    c = lax.broadcasted_iota(jnp.int32, (rows, rows), 1)
    return ((r >= c) & (r // blk == c // blk)).astype(BF16)


def _split_lanes(x, n, w):
    s, c, _ = x.shape
    return jnp.stack([x[:, :, j * w:(j + 1) * w] for j in range(n)], axis=1).reshape(s * n, c, w)


def _merge_lanes(x, n):
    sn, c, w = x.shape
    x = x.reshape(sn // n, n, c, w)
    return jnp.concatenate([x[:, j] for j in range(n)], axis=-1).reshape(sn // n * c, n * w)


def _inproj_body(x_ref, g_ref, w_ref, o_ref, u_scr):
    @pl.when(pl.program_id(1) == 0)
    def _():
        x = x_ref[...]
        ms = jnp.mean(x * x, axis=-1, keepdims=True)
        u_scr[...] = (x * lax.rsqrt(ms + NORM_EPS) * g_ref[...]).astype(BF16)

    o_ref[...] = _dot(u_scr[...], w_ref[...])


def _inproj(hp, g, w, tm, tn):
    m = hp.shape[0]
    return pl.pallas_call(
        _inproj_body,
        grid=(m // tm, NP_COLS // tn),
        in_specs=[
            pl.BlockSpec((tm, D_MODEL), lambda i, j: (i, 0)),
            pl.BlockSpec((1, D_MODEL), lambda i, j: (0, 0)),
            pl.BlockSpec((D_MODEL, tn), lambda i, j: (0, j)),
        ],
        out_specs=pl.BlockSpec((tm, tn), lambda i, j: (i, j)),
        out_shape=jax.ShapeDtypeStruct((m, NP_COLS), F32),
        scratch_shapes=[pltpu.VMEM((tm, D_MODEL), BF16)],
        compiler_params=pltpu.CompilerParams(
            dimension_semantics=("parallel", "arbitrary"), vmem_limit_bytes=VMEM_LIMIT),
        name="inproj",
    )(hp, g, w)


def _gla_body(q_ref, k_ref, v_ref, r_ref, sm_ref, gate_ref, wa2_ref, ba_ref, ng_ref, wo_ref,
              o_ref, s_scr):
    c = pl.program_id(1)
    nb = q_ref.shape[0]
    C = CHUNK
    rows = nb * C
    H = GLA_HEADS

    @pl.when(c == 0)
    def _():
        s_scr[...] = jnp.zeros_like(s_scr)

    row = lax.broadcasted_iota(jnp.int32, (rows, 1), 0) % C
    causal = (lax.broadcasted_iota(jnp.int32, (C, C), 0) >= lax.broadcasted_iota(jnp.int32, (C, C), 1))[None]

    z = _dot(sm_ref[...].reshape(rows, SMALL_W)[:, 0:LANES].astype(BF16), wa2_ref[...]) + ba_ref[...]
    gk = -_softplus(-z) * (1.0 / GLA_GATE_NORM)
    gk = jnp.where((c > 0) | (row >= FRONT_PAD), gk, 0.0)
    b = _cumsum_rows(gk, _tri_blocks(rows, C)).reshape(nb, C, GLA_KEY)
    b_ref = b[:, C // 2 - 1:C // 2, :]
    b_last = b[:, C - 1:C, :]

    q = q_ref[...] * (GLA_DK ** -0.5)
    k = k_ref[...]
    keys = lambda x: _split_lanes(x, H, GLA_DK)
    vals = lambda x: _split_lanes(x, H, GLA_DV)
    qe = keys((q * jnp.exp(b - b_ref)).astype(BF16))
    ke = keys((k * jnp.exp(b_ref - b)).astype(BF16))
    qs = keys((q * jnp.exp(b)).astype(BF16))
    kd = keys((k * jnp.exp(b_last - b)).astype(BF16))
    decay = keys(jnp.exp(b_last))
    v = vals(v_ref[...].astype(BF16))
    silu_r = r_ref[...]
    silu_r = vals(silu_r * jax.nn.sigmoid(silu_r))

    a = jnp.where(causal, _bmm_nt(qe, ke), 0.0)
    st = s_scr[...]
    o = _bmm(a.astype(BF16), v) + _bmm_nt(qs, st.astype(BF16))
    s_scr[...] = st * decay + _bmm_tn(v, kd)
    o = o * lax.rsqrt(jnp.mean(o * o, axis=-1, keepdims=True) + NORM_EPS) * ng_ref[...]
    og = _merge_lanes(o * silu_r, H).astype(BF16)
    out = jax.nn.sigmoid(gate_ref[...].reshape(rows, D_MODEL)) * _dot(og, wo_ref[...])
    o_ref[...] = out.reshape(nb, C, D_MODEL)


def _gla(p3, wa2, ba, ng, wo, nb):
    batch, lp, _ = p3.shape
    n_chunks = lp // CHUNK
    pspec = lambda w, col: pl.BlockSpec((nb, CHUNK, w), lambda b, c: (b, c, col // w))
    full = lambda shape: pl.BlockSpec(shape, lambda b, c: (0,) * len(shape))
    return pl.pallas_call(
        _gla_body,
        grid=(batch // nb, n_chunks),
        in_specs=[
            pspec(GLA_KEY, COL_GQ), pspec(GLA_KEY, COL_GK), pspec(D_MODEL, COL_GV),
            pspec(D_MODEL, COL_GR), pspec(SMALL_W, COL_SMALL), pspec(D_MODEL, COL_GATE_GLA),
            full((LANES, GLA_KEY)), full((1, GLA_KEY)), full((1, GLA_DV)), full((D_MODEL, D_MODEL)),
        ],
        out_specs=pl.BlockSpec((nb, CHUNK, D_MODEL), lambda b, c: (b, jnp.maximum(c - 1, 0), 0)),
        out_shape=jax.ShapeDtypeStruct((batch, (n_chunks - 1) * CHUNK, D_MODEL), F32),
        scratch_shapes=[pltpu.VMEM((nb * GLA_HEADS, GLA_DV, GLA_DK), F32)],
        compiler_params=pltpu.CompilerParams(
            dimension_semantics=("parallel", "arbitrary"), vmem_limit_bytes=VMEM_LIMIT),
        name="gla",
    )(p3, p3, p3, p3, p3, p3, wa2, ba, ng, wo)


def _stack_heads(x, lane_lo):
    return jnp.concatenate([jnp.where(lane_lo, x, 0.0), jnp.where(lane_lo, 0.0, x)], axis=1)


def _rwkv_body(pr_ref, pk_ref, pv_ref, sm_ref, gate_ref, mg_ref, x_ref,
               mur_ref, muk_ref, muv_ref, musm_ref, wd_ref, bd_ref, wa_ref, ba_ref, wg_ref,
               kk_ref, ka_ref, rk_ref, lng_ref, lnb_ref, wo_ref, wout_ref,
               o_ref,
               shr_scr, shk_scr, shv_scr, shs_scr, s_scr):
    c = pl.program_id(1)
    nb = pr_ref.shape[0]
    C = CHUNK
    rows = nb * C
    G = nb * RWKV_PAIRS

    @pl.when(c == 0)
    def _():
        s_scr[...] = jnp.zeros_like(s_scr)
        for scr in (shr_scr, shk_scr, shv_scr, shs_scr):
            scr[:, 7:8, :] = jnp.zeros((nb, 1, scr.shape[2]), F32)

    def lerp(x_ref_, scr, mu_ref):
        x = x_ref_[...]
        scr[:, 8:8 + C, :] = x
        prev = scr[:, 7:7 + C, :]
        scr[:, 7:8, :] = x[:, C - 1:C, :]
        return x + (prev - x) * mu_ref[...]

    pairs = lambda x: _split_lanes(x, RWKV_PAIRS, LANES)
    per_pair = lambda ref: jnp.broadcast_to(ref[...][None], (nb, RWKV_PAIRS, 1, LANES)).reshape(G, 1, LANES)

    r = lerp(pr_ref, shr_scr, mur_ref)
    k = lerp(pk_ref, shk_scr, muk_ref)
    v = lerp(pv_ref, shv_scr, muv_ref)
    ls = lerp(sm_ref, shs_scr, musm_ref).reshape(rows, SMALL_W)
    s1 = ls[:, LANES:2 * LANES]
    xw = bd_ref[...] + _dot(jnp.tanh(s1).astype(BF16), wd_ref[...])
    logw = -jnp.exp(-_softplus(-xw) - 0.5)
    a = jax.nn.sigmoid(ba_ref[...] + _dot(s1.astype(BF16), wa_ref[...]))
    g = _dot(jax.nn.sigmoid(ls[:, 2 * LANES:4 * LANES]).astype(BF16), wg_ref[...])

    cl = _cumsum_rows(logw, _tri_blocks(rows, C)).reshape(nb, C, D_MODEL)
    logw = logw.reshape(nb, C, D_MODEL)
    cref = cl[:, C // 2 - 1:C // 2, :]
    clast = cl[:, C - 1:C, :]
    e_neg = pairs(jnp.exp(cref - cl))
    e_prev = pairs(jnp.exp(cl - logw - cref))
    e_cur = pairs(jnp.exp(cl - cref))
    g_last = pairs(jnp.exp(clast - cref))
    g_ref = pairs(jnp.exp(cref))
    g_c = pairs(jnp.exp(clast))
    r = pairs(r)
    k = pairs(k)
    v = pairs(v)
    a = pairs(a.reshape(nb, C, D_MODEL))

    lane_lo = lax.broadcasted_iota(jnp.int32, (1, C, LANES), 2) < RWKV_HEAD
    er = lax.broadcasted_iota(jnp.int32, (LANES, LANES), 0)
    ec = lax.broadcasted_iota(jnp.int32, (LANES, LANES), 1)
    seg_ones = ((er // RWKV_HEAD) == (ec // RWKV_HEAD)).astype(BF16)
    strict = ((er % C) > (ec % C))[None]
    incl = ((er % C) >= (ec % C))[None]

    def seg(x):
        hi, lo = _split2(x.reshape(G * C, LANES))
        return (_dot(hi, seg_ones) + _dot(lo, seg_ones)).reshape(G, C, LANES)

    kk = k * per_pair(kk_ref)
    kk = kk / jnp.maximum(jnp.sqrt(seg(kk * kk)), 1e-12)
    k2 = k * (1.0 + (a - 1.0) * per_pair(ka_ref))
    bonus = seg(r * k2 * per_pair(rk_ref))
    at = -kk * e_prev
    rt = r * e_cur
    bt = kk * a * e_neg
    kt = k2 * e_neg
    st = s_scr[...]
    sp = (st * g_ref).astype(BF16)

    lhs = jnp.concatenate([_stack_heads(at, lane_lo), _stack_heads(rt, lane_lo)], axis=1).astype(BF16)
    rhs = jnp.concatenate([_stack_heads(bt, lane_lo), _stack_heads(kt, lane_lo)], axis=1).astype(BF16)
    m1 = _bmm_nt(lhs, rhs)
    n2 = 2 * C
    ab = jnp.where(strict, m1[:, :n2, :n2], 0.0).astype(BF16)
    ak = jnp.where(strict, m1[:, :n2, n2:], 0.0).astype(BF16)
    rb = jnp.where(incl, m1[:, n2:, :n2], 0.0).astype(BF16)
    rk = jnp.where(incl, m1[:, n2:, n2:], 0.0).astype(BF16)
    vs = _stack_heads(v, lane_lo).astype(BF16)

    pm = _bmm_nt(lhs[:, :n2], sp) + _bmm(ak, vs)
    x = ab
    for i in range(6):
        pm = pm + _bmm(x, pm.astype(BF16))
        if i < 5:
            x = _bmm(x, x).astype(BF16)
    pmb = pm.astype(BF16)
    y2 = _bmm_nt(lhs[:, n2:], sp) + _bmm(rb, pmb) + _bmm(rk, vs)
    y = y2[:, :C] + y2[:, C:]

    upd_l = jnp.concatenate([pmb, vs], axis=1)
    upd_r = jnp.concatenate([_stack_heads(bt * g_last, lane_lo),
                             _stack_heads(kt * g_last, lane_lo)], axis=1).astype(BF16)
    s_scr[...] = st * g_c + _bmm_tn(upd_l, upd_r)

    mean = seg(y) * (1.0 / RWKV_HEAD)
    yc = y - mean
    var = seg(yc * yc) * (1.0 / RWKV_HEAD)
    y = yc * lax.rsqrt(var + RWKV_GN_EPS) * per_pair(lng_ref) + per_pair(lnb_ref) + bonus * v

    y = _merge_lanes(y, RWKV_PAIRS)
    orw = _dot((y * g).astype(BF16), wo_ref[...])
    mix = mg_ref[...].reshape(rows, D_MODEL) + jax.nn.sigmoid(gate_ref[...].reshape(rows, D_MODEL)) * orw
    out = x_ref[...].reshape(rows, D_MODEL) + _dot(mix.astype(BF16), wout_ref[...])
    o_ref[...] = out.reshape(nb, C, D_MODEL)


def _rwkv(p3, mg, x, prm, nb):
    batch, lp, _ = p3.shape
    n_chunks = lp // CHUNK
    real = lambda b, c: (b, jnp.maximum(c - 1, 0), 0)
    pspec = lambda w, col: pl.BlockSpec((nb, CHUNK, w), lambda b, c: (b, c, col // w))
    full = lambda shape: pl.BlockSpec(shape, lambda b, c: (0,) * len(shape))
    pairvec = full((RWKV_PAIRS, 1, LANES))
    vec = full((1, D_MODEL))
    shift = lambda w: pltpu.VMEM((nb, CHUNK + SUBLANES, w), F32)
    return pl.pallas_call(
        _rwkv_body,
        grid=(batch // nb, n_chunks),
        in_specs=[
            pspec(D_MODEL, COL_RR), pspec(D_MODEL, COL_RK), pspec(D_MODEL, COL_RV),
            pspec(SMALL_W, COL_SMALL), pspec(D_MODEL, COL_GATE_RWKV),
            pl.BlockSpec((nb, CHUNK, D_MODEL), real), pl.BlockSpec((nb, CHUNK, D_MODEL), real),
            vec, vec, vec, full((1, SMALL_W)),
            full((LANES, D_MODEL)), vec, full((LANES, D_MODEL)), vec, full((2 * LANES, D_MODEL)),
            pairvec, pairvec, pairvec, pairvec, pairvec,
            full((D_MODEL, D_MODEL)), full((D_MODEL, D_MODEL)),
        ],
        out_specs=pl.BlockSpec((nb, CHUNK, D_MODEL), real),
        out_shape=jax.ShapeDtypeStruct(x.shape, F32),
        scratch_shapes=[shift(D_MODEL), shift(D_MODEL), shift(D_MODEL), shift(SMALL_W),
                        pltpu.VMEM((nb * RWKV_PAIRS, LANES, LANES), F32)],
        compiler_params=pltpu.CompilerParams(
            dimension_semantics=("parallel", "arbitrary"), vmem_limit_bytes=VMEM_LIMIT),
        name="rwkv",
    )(p3, p3, p3, p3, p3, mg, x, *prm)


def _route_body(h_ref, g_ref, wr_ref, br_ref, idx_ref, prob_ref, rank_ref, cnt_ref, base_scr):
    tm = h_ref.shape[0]

    @pl.when(pl.program_id(0) == 0)
    def _():
        base_scr[...] = jnp.zeros_like(base_scr)

    h = h_ref[...]
    u = h * lax.rsqrt(jnp.mean(h * h, axis=-1, keepdims=True) + NORM_EPS) * g_ref[...]
    hi, mid, lo = _split3(u)
    whi, wmid, wlo = wr_ref[0], wr_ref[1], wr_ref[2]
    logits = (_dot(hi, whi) + (_dot(hi, wmid) + _dot(mid, whi))
              + (_dot(hi, wlo) + _dot(mid, wmid) + _dot(lo, whi))) + br_ref[...]
    lane = lax.broadcasted_iota(jnp.int32, (tm, LANES), 1)
    logits = jnp.where(lane < N_EXPERTS, logits, -jnp.inf)
    idx_out = jnp.zeros((tm, LANES), jnp.int32)
    prob_out = jnp.zeros((tm, LANES), F32)
    denom = jnp.zeros((tm, 1), F32)
    hits = []
    top = None
    for k in range(TOP_K):
        m = jnp.max(logits, axis=-1, keepdims=True)
        idx = jnp.min(jnp.where(logits == m, lane, LANES), axis=-1, keepdims=True)
        hit = lane == idx
        top = m if top is None else top
        w = jnp.exp(m - top)
        idx_out = jnp.where(lane == k, idx, idx_out)
        prob_out = jnp.where(lane == k, w, prob_out)
        denom = denom + w
        hits.append(hit)
        logits = jnp.where(hit, -jnp.inf, logits)
    idx_ref[...] = idx_out
    prob_ref[...] = prob_out / denom

    tot = jnp.zeros((tm, LANES), F32)
    for hit in hits:
        tot = tot + jnp.where(hit, 1.0, 0.0)
    r = lax.broadcasted_iota(jnp.int32, (tm, tm), 0)
    c = lax.broadcasted_iota(jnp.int32, (tm, tm), 1)
    before = _dot((r > c).astype(BF16), tot.astype(BF16)) + base_scr[...]
    rank_out = jnp.zeros((tm, LANES), F32)
    for k, hit in enumerate(hits):
        rk = jnp.sum(jnp.where(hit, before, 0.0), axis=-1, keepdims=True)
        rank_out = jnp.where(lane == k, rk, rank_out)
    rank_ref[...] = rank_out.astype(jnp.int32)
    base_scr[...] += jnp.sum(tot, axis=0, keepdims=True)
    cnt_ref[...] = base_scr[...]


def _route(h2, g, wr3, br, tm):
    t = h2.shape[0]
    full = lambda shape: pl.BlockSpec(shape, lambda i: (0,) * len(shape))
    tile = lambda w: pl.BlockSpec((tm, w), lambda i: (i, 0))
    return pl.pallas_call(
        _route_body,
        grid=(t // tm,),
        in_specs=[tile(D_MODEL), full((1, D_MODEL)), full((3, D_MODEL, LANES)), full((1, LANES))],
        out_specs=[tile(LANES), tile(LANES), tile(LANES), full((1, LANES))],
        out_shape=[jax.ShapeDtypeStruct((t, LANES), jnp.int32),
                   jax.ShapeDtypeStruct((t, LANES), F32),
                   jax.ShapeDtypeStruct((t, LANES), jnp.int32),
                   jax.ShapeDtypeStruct((1, LANES), F32)],
        scratch_shapes=[pltpu.VMEM((1, LANES), F32)],
        compiler_params=pltpu.CompilerParams(
            dimension_semantics=("arbitrary",), vmem_limit_bytes=VMEM_LIMIT),
        name="route",
    )(h2, g, wr3, br)


def _dispatch_body(tail_ref, dest_ref, h_ref, xs_ref, hs_scr, sem):
    tm = h_ref.shape[0]
    zrows = hs_scr.shape[0]

    @pl.when(pl.program_id(0) == 0)
    def _():
        hs_scr[...] = jnp.zeros_like(hs_scr)
        for e in range(N_EXPERTS):
            start = pl.multiple_of(tail_ref[e] * SUBLANES, SUBLANES)
            pltpu.make_async_copy(hs_scr, xs_ref.at[pl.ds(start, zrows)], sem).start()
        for e in range(N_EXPERTS):
            pltpu.make_async_copy(hs_scr, xs_ref.at[pl.ds(0, zrows)], sem).wait()

        def zero_tile(j, carry):
            start = pl.multiple_of(j * zrows, zrows)
            pltpu.make_async_copy(hs_scr, xs_ref.at[pl.ds(start, zrows)], sem).start()
            return carry

        def wait_tile(j, carry):
            pltpu.make_async_copy(hs_scr, xs_ref.at[pl.ds(0, zrows)], sem).wait()
            return carry

        n_tiles = xs_ref.shape[0] // zrows
        lax.fori_loop(tail_ref[N_EXPERTS], n_tiles, zero_tile, 0)
        lax.fori_loop(tail_ref[N_EXPERTS], n_tiles, wait_tile, 0)

    for c in range(SUBLANES):
        hs_scr[pl.ds(c, tm, stride=SUBLANES), :] = h_ref[:, c * LANES:(c + 1) * LANES]

    def issue(t, carry):
        src = hs_scr.at[pl.ds(pl.multiple_of(t * SUBLANES, SUBLANES), SUBLANES)]
        for k in range(TOP_K):
            d = pl.multiple_of(dest_ref[0, t * TOP_K + k] * SUBLANES, SUBLANES)
            pltpu.make_async_copy(src, xs_ref.at[pl.ds(d, SUBLANES)], sem).start()
        return carry

    lax.fori_loop(0, tm, issue, 0, unroll=DMA_UNROLL)
    for k in range(TOP_K):
        pltpu.make_async_copy(hs_scr, xs_ref.at[pl.ds(0, tm * SUBLANES)], sem).wait()


def _dispatch(tail, dest2, h2, ns, tm):
    t = h2.shape[0]
    return pl.pallas_call(
        _dispatch_body,
        grid_spec=pltpu.PrefetchScalarGridSpec(
            num_scalar_prefetch=1,
            grid=(t // tm,),
            in_specs=[pl.BlockSpec((None, 1, tm * TOP_K), lambda i, tl: (i, 0, 0), memory_space=pltpu.SMEM),
                      pl.BlockSpec((tm, D_MODEL), lambda i, tl: (i, 0))],
            out_specs=pl.BlockSpec(memory_space=pl.ANY),
            scratch_shapes=[pltpu.VMEM((tm * SUBLANES, LANES), F32), pltpu.SemaphoreType.DMA(())],
        ),
        out_shape=jax.ShapeDtypeStruct((ns * SUBLANES, LANES), F32),
        compiler_params=pltpu.CompilerParams(
            dimension_semantics=("arbitrary",), vmem_limit_bytes=VMEM_LIMIT),
        name="dispatch",
    )(tail, dest2, h2)


def _experts_body(te_ref, nu_ref, xs_ref, g_ref, wgu_ref, bgu_ref, wdn_ref, bdn_ref, ys_ref,
                  wgu_scr, wdn_scr):
    i = pl.program_id(0)
    tm = xs_ref.shape[0] // SUBLANES

    @pl.when((i == 0) | (te_ref[i] != te_ref[jnp.maximum(i - 1, 0)]))
    def _():
        wgu_scr[...] = wgu_ref[0].astype(BF16)
        wdn_scr[...] = wdn_ref[0].astype(BF16)

    @pl.when(i < nu_ref[0])
    def _():
        x = jnp.concatenate([xs_ref[pl.ds(c, tm, stride=SUBLANES), :] for c in range(SUBLANES)], axis=1)
        u = x * lax.rsqrt(jnp.mean(x * x, axis=-1, keepdims=True) + NORM_EPS) * g_ref[...]
        hgu = _dot(u.astype(BF16), wgu_scr[...]) + bgu_ref[0]
        d = wdn_scr.shape[0]
        gate = jnp.minimum(hgu[:, :d], SWIGLU_LIMIT)
        up = jnp.clip(hgu[:, d:], -SWIGLU_LIMIT, SWIGLU_LIMIT)
        act = (up + 1.0) * (gate * jax.nn.sigmoid(gate * SWIGLU_ALPHA))
        y = _dot(act.astype(BF16), wdn_scr[...]) + bdn_ref[0]
        for c in range(SUBLANES):
            ys_ref[pl.ds(c, tm, stride=SUBLANES), :] = y[:, c * LANES:(c + 1) * LANES]

    @pl.when(i >= nu_ref[0])
    def _():
        ys_ref[...] = jnp.zeros_like(ys_ref)


def _experts(tile_expert, n_used, xs, g, wgu, bgu, wdn, bdn, tm):
    n_tiles = xs.shape[0] // (tm * SUBLANES)
    wsel = lambda shape: pl.BlockSpec(shape, lambda i, te, nu: (te[i], 0, 0))
    used = lambda i, te, nu: (jnp.minimum(i, nu[0] - 1), 0)
    return pl.pallas_call(
        _experts_body,
        grid_spec=pltpu.PrefetchScalarGridSpec(
            num_scalar_prefetch=2,
            grid=(n_tiles,),
            in_specs=[pl.BlockSpec((tm * SUBLANES, LANES), used),
                      pl.BlockSpec((1, D_MODEL), lambda i, te, nu: (0, 0)),
                      wsel((1, D_MODEL, 2 * D_MODEL)), wsel((1, 1, 2 * D_MODEL)),
                      wsel((1, D_MODEL, D_MODEL)), wsel((1, 1, D_MODEL))],
            out_specs=pl.BlockSpec((tm * SUBLANES, LANES), lambda i, te, nu: (i, 0)),
            scratch_shapes=[pltpu.VMEM((D_MODEL, 2 * D_MODEL), BF16), pltpu.VMEM((D_MODEL, D_MODEL), BF16)],
        ),
        out_shape=jax.ShapeDtypeStruct(xs.shape, F32),
        compiler_params=pltpu.CompilerParams(
            dimension_semantics=("arbitrary",), vmem_limit_bytes=VMEM_LIMIT),
        name="experts",
    )(tile_expert, n_used, xs, g, wgu, bgu, wdn, bdn)


def _combine_body(dcur_ref, dnext_ref, prob_ref, h_ref, gf_ref, ys_ref, o_ref, ybuf, sem):
    i = pl.program_id(0)
    n = pl.num_programs(0)
    tm = h_ref.shape[0]
    slot = i % 2

    def gather(dref, s):
        def issue(t, carry):
            row = pl.multiple_of(t * SUBLANES, SUBLANES)
            for k in range(TOP_K):
                d = pl.multiple_of(dref[0, t * TOP_K + k] * SUBLANES, SUBLANES)
                pltpu.make_async_copy(ys_ref.at[pl.ds(d, SUBLANES)], ybuf.at[s, k, pl.ds(row, SUBLANES)],
                                      sem.at[s]).start()
            return carry
        lax.fori_loop(0, tm, issue, 0, unroll=DMA_UNROLL)

    @pl.when(i == 0)
    def _():
        gather(dcur_ref, 0)

    @pl.when(i + 1 < n)
    def _():
        gather(dnext_ref, 1 - slot)

    for k in range(TOP_K):
        pltpu.make_async_copy(ys_ref.at[pl.ds(0, tm * SUBLANES)], ybuf.at[slot, k], sem.at[slot]).wait()
    prob = prob_ref[...]
    cols = []
    for c in range(SUBLANES):
        acc = h_ref[:, c * LANES:(c + 1) * LANES]
        for k in range(TOP_K):
            acc = acc + prob[:, k:k + 1] * ybuf[slot, k, pl.ds(c, tm, stride=SUBLANES), :]
        cols.append(acc)
    h = jnp.concatenate(cols, axis=1)
    o_ref[...] = h * lax.rsqrt(jnp.mean(h * h, axis=-1, keepdims=True) + NORM_EPS) * gf_ref[...]


def _combine(dest2, prob, h2, gf, ys, tm):
    t = h2.shape[0]
    n = t // tm
    smem = lambda imap: pl.BlockSpec((None, 1, tm * TOP_K), imap, memory_space=pltpu.SMEM)
    return pl.pallas_call(
        _combine_body,
        grid=(n,),
        in_specs=[smem(lambda i: (i, 0, 0)), smem(lambda i: (jnp.minimum(i + 1, n - 1), 0, 0)),
                  pl.BlockSpec((tm, LANES), lambda i: (i, 0)),
                  pl.BlockSpec((tm, D_MODEL), lambda i: (i, 0)),
                  pl.BlockSpec((1, D_MODEL), lambda i: (0, 0)),
                  pl.BlockSpec(memory_space=pl.ANY)],
        out_specs=pl.BlockSpec((tm, D_MODEL), lambda i: (i, 0)),
        out_shape=jax.ShapeDtypeStruct((t, D_MODEL), F32),
        scratch_shapes=[pltpu.VMEM((2, TOP_K, tm * SUBLANES, LANES), F32), pltpu.SemaphoreType.DMA((2,))],
        compiler_params=pltpu.CompilerParams(
            dimension_semantics=("arbitrary",), vmem_limit_bytes=VMEM_LIMIT),
        name="combine",
    )(dest2, dest2, prob, h2, gf, ys)


def _moe(h2, g, wr3, br, wgu, bgu, wdn, bdn, gf):
    t = h2.shape[0]
    tm_x = _pick_tile(t, (512, 256, 128))
    tm_c = _pick_tile(t, (256, 128))
    idx, prob, rank, cnt = _route(h2, g, wr3, br, tm_x)

    counts = cnt[0, :N_EXPERTS].astype(jnp.int32)
    padded = (counts + tm_x - 1) // tm_x * tm_x
    ends = jnp.cumsum(padded)
    offs = ends - padded
    ns = t * TOP_K + (N_EXPERTS + 1) * tm_x
    e_ids = jnp.arange(N_EXPERTS, dtype=jnp.int32)
    idx4 = idx[:, :TOP_K]
    dest = rank[:, :TOP_K] + jnp.sum(jnp.where(idx4[..., None] == e_ids, offs, 0), axis=-1)
    tile_start = jnp.arange(ns // tm_x, dtype=jnp.int32) * tm_x
    tile_expert = jnp.minimum(jnp.sum((tile_start[:, None] >= ends[None, :]).astype(jnp.int32), axis=1),
                              N_EXPERTS - 1)
    n_used = (ends[-1] // tm_x).reshape(1)

    xs = _dispatch(jnp.concatenate([offs + counts, n_used]), dest.reshape(t // tm_x, 1, tm_x * TOP_K), h2, ns, tm_x)
    ys = _experts(tile_expert, n_used, xs, g, wgu, bgu, wdn, bdn, tm_x)
    return _combine(dest.reshape(t // tm_c, 1, tm_c * TOP_K), prob, h2, gf, ys, tm_c)


def _pick_tile(n, prefs):
    for t in prefs:
        if n % t == 0:
            return t
    return n


def _pad_rows(w, rows, offset=0):
    out = jnp.zeros((rows, w.shape[1]), w.dtype)
    return out.at[offset:offset + w.shape[0]].set(w)


def kernel(x, meta_tokens, norm_mix_g, w_in, w_gla_a2, b_gla_a, gla_norm_g, w_gla_o, mu_r, mu_k, mu_v, mu_w, mu_a, mu_g, w_decay2, b_decay, w_a2, b_a, w_gate2, k_k, k_a, r_k, ln_x_g, ln_x_b, w_rwkv_o, w_out, norm_ffn_g, w_router, b_router, w_exp_gu, b_exp_gu, w_exp_down, b_exp_down, norm_final_g):
    batch, seq, d = x.shape
    assert d == D_MODEL and seq % CHUNK == 0 and w_in.shape[0] == 1
    lp = FRONT_PAD + N_META + seq
    row = lambda a: a.reshape(1, -1)

    meta = jnp.broadcast_to(meta_tokens[None].astype(x.dtype), (batch, N_META, d))
    hp = jnp.concatenate([jnp.zeros((batch, FRONT_PAD, d), x.dtype), meta, x], axis=1).reshape(batch * lp, d)

    splits = (GLA_KEY, GLA_KEY, D_MODEL, D_MODEL, GLA_GATE_RANK, D_MODEL, D_MODEL, D_MODEL,
              RWKV_DECAY_RANK, RWKV_A_RANK, RWKV_GATE_RANK, D_MODEL, D_MODEL)
    offs = [0]
    for s in splits:
        offs.append(offs[-1] + s)
    piece = lambda i: w_in[0][:, offs[i]:offs[i + 1]]
    zcols = lambda n: jnp.zeros((d, n), w_in.dtype)
    small = jnp.concatenate([piece(4), zcols(LANES - GLA_GATE_RANK), piece(8), piece(9), piece(10),
                             zcols(2 * LANES - RWKV_GATE_RANK)], axis=1)
    w_big = jnp.concatenate([piece(0), piece(1), piece(2), piece(3), piece(5), piece(6), piece(7),
                             piece(11), piece(12), small], axis=1).astype(BF16)

    m = batch * lp
    tm = _pick_tile(m, (1280, 640, 512, 320, 256, 192, 128, 64))
    tn = _pick_tile(NP_COLS, (2176, 512))
    p = _inproj(hp, row(norm_mix_g[0]), w_big, tm, tn)

    p3 = p.reshape(batch, lp, NP_COLS)
    nb = _pick_tile(batch, (SEQS_PER_STEP, 1))
    mg = _gla(p3, _pad_rows(w_gla_a2[0], LANES).astype(BF16), row(b_gla_a[0]), row(gla_norm_g[0]),
              w_gla_o[0].astype(BF16), nb)

    pv = lambda a: a.reshape(RWKV_PAIRS, 1, LANES)
    mu_small = jnp.concatenate([jnp.zeros((LANES,), F32), mu_w[0], mu_a[0], mu_g[0],
                                jnp.zeros((2 * LANES - RWKV_GATE_RANK,), F32)])
    prm = (row(mu_r[0]), row(mu_k[0]), row(mu_v[0]), row(mu_small),
           _pad_rows(w_decay2[0], LANES).astype(BF16), row(b_decay[0]),
           _pad_rows(w_a2[0], LANES, RWKV_DECAY_RANK).astype(BF16), row(b_a[0]),
           _pad_rows(w_gate2[0], 2 * LANES).astype(BF16),
           pv(k_k[0]), pv(k_a[0]), pv(r_k[0]), pv(ln_x_g[0]), pv(ln_x_b[0]),
           w_rwkv_o[0].astype(BF16), w_out[0].astype(BF16))
    h2 = _rwkv(p3, mg, x, prm, nb).reshape(batch * seq, d)

    wr = jnp.zeros((d, LANES), F32).at[:, :N_EXPERTS].set(w_router[0])
    wr3 = jnp.stack(_split3(wr))
    br = jnp.zeros((1, LANES), F32).at[0, :N_EXPERTS].set(b_router[0])
    out = _moe(h2, row(norm_ffn_g[0]), wr3, br, w_exp_gu[0], b_exp_gu[0][:, None, :],
               w_exp_down[0], b_exp_down[0][:, None, :], row(norm_final_g))
    return out.reshape(batch, seq, d)
```

```python
import jax
import jax.numpy as jnp
from jax import lax
from jax.experimental import pallas as pl
from jax.experimental.pallas import tpu as pltpu

F32 = jnp.float32
BF16 = jnp.bfloat16

D_MODEL = 1024
N_META = 16
NORM_EPS = 1e-5
CHUNK = 64
FRONT_PAD = (-N_META) % CHUNK
GLA_HEADS = 4
GLA_DK = 128
GLA_DV = 256
GLA_KEY = GLA_HEADS * GLA_DK
GLA_GATE_RANK = 16
GLA_GATE_NORM = 16.0
RWKV_HEAD = 64
RWKV_PAIRS = D_MODEL // (2 * RWKV_HEAD)
RWKV_DECAY_RANK = 64
RWKV_A_RANK = 64
RWKV_GATE_RANK = 160
RWKV_GN_EPS = 64e-5
N_EXPERTS = 32
TOP_K = 4
SWIGLU_LIMIT = 7.0
SWIGLU_ALPHA = 1.702
SEQS_PER_STEP = 2
DMA_UNROLL = 8
LANES = 128
SUBLANES = 8
SMALL_W = 512
COL_GQ, COL_GK, COL_GV, COL_GR = 0, 512, 1024, 2048
COL_RR, COL_RK, COL_RV = 3072, 4096, 5120
COL_GATE_GLA, COL_GATE_RWKV, COL_SMALL = 6144, 7168, 8192
NP_COLS = COL_SMALL + SMALL_W
VMEM_LIMIT = 56 * 1024 * 1024


def _dot(a, b):
    return jnp.dot(a, b, preferred_element_type=F32)


def _bmm(a, b):
    return jnp.einsum("gik,gkj->gij", a, b, preferred_element_type=F32)


def _bmm_nt(a, b):
    return jnp.einsum("gik,gjk->gij", a, b, preferred_element_type=F32)


def _bmm_tn(a, b):
    return jnp.einsum("gti,gtj->gij", a, b, preferred_element_type=F32)


def _split2(x):
    hi = x.astype(BF16)
    lo = (x - hi.astype(F32)).astype(BF16)
    return hi, lo


def _split3(x):
    hi = x.astype(BF16)
    r1 = x - hi.astype(F32)
    mid = r1.astype(BF16)
    lo = (r1 - mid.astype(F32)).astype(BF16)
    return hi, mid, lo


def _cumsum_rows(x, tri):
    hi, mid, lo = _split3(x)
    return _dot(tri, hi) + _dot(tri, mid) + _dot(tri, lo)


def _softplus(x):
    return jnp.maximum(x, 0.0) + jnp.log1p(jnp.exp(-jnp.abs(x)))


def _tri_blocks(rows, blk):
    r = lax.broadcasted_iota(jnp.int32, (rows, rows), 0)
    c = lax.broadcasted_iota(jnp.int32, (rows, rows), 1)
    return ((r >= c) & (r // blk == c // blk)).astype(BF16)


def _split_lanes(x, n, w):
    s, c, _ = x.shape
    return jnp.stack([x[:, :, j * w:(j + 1) * w] for j in range(n)], axis=1).reshape(s * n, c, w)


def _merge_lanes(x, n):
    sn, c, w = x.shape
    x = x.reshape(sn // n, n, c, w)
    return jnp.concatenate([x[:, j] for j in range(n)], axis=-1).reshape(sn // n * c, n * w)


def _inproj_body(x_ref, g_ref, w_ref, o_ref, u_scr):
    @pl.when(pl.program_id(1) == 0)
    def _():
        x = x_ref[...]
        ms = jnp.mean(x * x, axis=-1, keepdims=True)
        u_scr[...] = (x * lax.rsqrt(ms + NORM_EPS) * g_ref[...]).astype(BF16)

    o_ref[...] = _dot(u_scr[...], w_ref[...])


def _inproj(hp, g, w, tm, tn):
    m = hp.shape[0]
    return pl.pallas_call(
        _inproj_body,
        grid=(m // tm, NP_COLS // tn),
        in_specs=[
            pl.BlockSpec((tm, D_MODEL), lambda i, j: (i, 0)),
            pl.BlockSpec((1, D_MODEL), lambda i, j: (0, 0)),
            pl.BlockSpec((D_MODEL, tn), lambda i, j: (0, j)),
        ],
        out_specs=pl.BlockSpec((tm, tn), lambda i, j: (i, j)),
        out_shape=jax.ShapeDtypeStruct((m, NP_COLS), F32),
        scratch_shapes=[pltpu.VMEM((tm, D_MODEL), BF16)],
        compiler_params=pltpu.CompilerParams(
            dimension_semantics=("parallel", "arbitrary"), vmem_limit_bytes=VMEM_LIMIT),
        name="inproj",
    )(hp, g, w)


def _gla_body(q_ref, k_ref, v_ref, r_ref, sm_ref, gate_ref, wa2_ref, ba_ref, ng_ref, wo_ref,
              o_ref, s_scr):
    c = pl.program_id(1)
    nb = q_ref.shape[0]
    C = CHUNK
    rows = nb * C
    H = GLA_HEADS

    @pl.when(c == 0)
    def _():
        s_scr[...] = jnp.zeros_like(s_scr)

    row = lax.broadcasted_iota(jnp.int32, (rows, 1), 0) % C
    causal = (lax.broadcasted_iota(jnp.int32, (C, C), 0) >= lax.broadcasted_iota(jnp.int32, (C, C), 1))[None]

    z = _dot(sm_ref[...].reshape(rows, SMALL_W)[:, 0:LANES].astype(BF16), wa2_ref[...]) + ba_ref[...]
    gk = -_softplus(-z) * (1.0 / GLA_GATE_NORM)
    gk = jnp.where((c > 0) | (row >= FRONT_PAD), gk, 0.0)
    b = _cumsum_rows(gk, _tri_blocks(rows, C)).reshape(nb, C, GLA_KEY)
    b_ref = b[:, C // 2 - 1:C // 2, :]
    b_last = b[:, C - 1:C, :]

    q = q_ref[...] * (GLA_DK ** -0.5)
    k = k_ref[...]
    keys = lambda x: _split_lanes(x, H, GLA_DK)
    vals = lambda x: _split_lanes(x, H, GLA_DV)
    qe = keys((q * jnp.exp(b - b_ref)).astype(BF16))
    ke = keys((k * jnp.exp(b_ref - b)).astype(BF16))
    qs = keys((q * jnp.exp(b)).astype(BF16))
    kd = keys((k * jnp.exp(b_last - b)).astype(BF16))
    decay = keys(jnp.exp(b_last))
    v = vals(v_ref[...].astype(BF16))
    silu_r = r_ref[...]
    silu_r = vals(silu_r * jax.nn.sigmoid(silu_r))

    a = jnp.where(causal, _bmm_nt(qe, ke), 0.0)
    st = s_scr[...]
    o = _bmm(a.astype(BF16), v) + _bmm_nt(qs, st.astype(BF16))
    s_scr[...] = st * decay + _bmm_tn(v, kd)
    o = o * lax.rsqrt(jnp.mean(o * o, axis=-1, keepdims=True) + NORM_EPS) * ng_ref[...]
    og = _merge_lanes(o * silu_r, H).astype(BF16)
    out = jax.nn.sigmoid(gate_ref[...].reshape(rows, D_MODEL)) * _dot(og, wo_ref[...])
    o_ref[...] = out.reshape(nb, C, D_MODEL)


def _gla(p3, wa2, ba, ng, wo, nb):
    batch, lp, _ = p3.shape
    n_chunks = lp // CHUNK
    pspec = lambda w, col: pl.BlockSpec((nb, CHUNK, w), lambda b, c: (b, c, col // w))
    full = lambda shape: pl.BlockSpec(shape, lambda b, c: (0,) * len(shape))
    return pl.pallas_call(
        _gla_body,
        grid=(batch // nb, n_chunks),
        in_specs=[
            pspec(GLA_KEY, COL_GQ), pspec(GLA_KEY, COL_GK), pspec(D_MODEL, COL_GV),
            pspec(D_MODEL, COL_GR), pspec(SMALL_W, COL_SMALL), pspec(D_MODEL, COL_GATE_GLA),
            full((LANES, GLA_KEY)), full((1, GLA_KEY)), full((1, GLA_DV)), full((D_MODEL, D_MODEL)),
        ],
        out_specs=pl.BlockSpec((nb, CHUNK, D_MODEL), lambda b, c: (b, jnp.maximum(c - 1, 0), 0)),
        out_shape=jax.ShapeDtypeStruct((batch, (n_chunks - 1) * CHUNK, D_MODEL), F32),
        scratch_shapes=[pltpu.VMEM((nb * GLA_HEADS, GLA_DV, GLA_DK), F32)],
        compiler_params=pltpu.CompilerParams(
            dimension_semantics=("parallel", "arbitrary"), vmem_limit_bytes=VMEM_LIMIT),
        name="gla",
    )(p3, p3, p3, p3, p3, p3, wa2, ba, ng, wo)


def _stack_heads(x, lane_lo):
    return jnp.concatenate([jnp.where(lane_lo, x, 0.0), jnp.where(lane_lo, 0.0, x)], axis=1)


def _rwkv_body(pr_ref, pk_ref, pv_ref, sm_ref, gate_ref, mg_ref, x_ref,
               mur_ref, muk_ref, muv_ref, musm_ref, wd_ref, bd_ref, wa_ref, ba_ref, wg_ref,
               kk_ref, ka_ref, rk_ref, lng_ref, lnb_ref, wo_ref, wout_ref,
               o_ref,
               shr_scr, shk_scr, shv_scr, shs_scr, s_scr):
    c = pl.program_id(1)
    nb = pr_ref.shape[0]
    C = CHUNK
    rows = nb * C
    G = nb * RWKV_PAIRS

    @pl.when(c == 0)
    def _():
        s_scr[...] = jnp.zeros_like(s_scr)
        for scr in (shr_scr, shk_scr, shv_scr, shs_scr):
            scr[:, 7:8, :] = jnp.zeros((nb, 1, scr.shape[2]), F32)

    def lerp(x_ref_, scr, mu_ref):
        x = x_ref_[...]
        scr[:, 8:8 + C, :] = x
        prev = scr[:, 7:7 + C, :]
        scr[:, 7:8, :] = x[:, C - 1:C, :]
        return x + (prev - x) * mu_ref[...]

    pairs = lambda x: _split_lanes(x, RWKV_PAIRS, LANES)
    per_pair = lambda ref: jnp.broadcast_to(ref[...][None], (nb, RWKV_PAIRS, 1, LANES)).reshape(G, 1, LANES)

    r = lerp(pr_ref, shr_scr, mur_ref)
    k = lerp(pk_ref, shk_scr, muk_ref)
    v = lerp(pv_ref, shv_scr, muv_ref)
    ls = lerp(sm_ref, shs_scr, musm_ref).reshape(rows, SMALL_W)
    s1 = ls[:, LANES:2 * LANES]
    xw = bd_ref[...] + _dot(jnp.tanh(s1).astype(BF16), wd_ref[...])
    logw = -jnp.exp(-_softplus(-xw) - 0.5)
    a = jax.nn.sigmoid(ba_ref[...] + _dot(s1.astype(BF16), wa_ref[...]))
    g = _dot(jax.nn.sigmoid(ls[:, 2 * LANES:4 * LANES]).astype(BF16), wg_ref[...])

    cl = _cumsum_rows(logw, _tri_blocks(rows, C)).reshape(nb, C, D_MODEL)
    logw = logw.reshape(nb, C, D_MODEL)
    cref = cl[:, C // 2 - 1:C // 2, :]
    clast = cl[:, C - 1:C, :]
    e_neg = pairs(jnp.exp(cref - cl))
    e_prev = pairs(jnp.exp(cl - logw - cref))
    e_cur = pairs(jnp.exp(cl - cref))
    g_last = pairs(jnp.exp(clast - cref))
    g_ref = pairs(jnp.exp(cref))
    g_c = pairs(jnp.exp(clast))
    r = pairs(r)
    k = pairs(k)
    v = pairs(v)
    a = pairs(a.reshape(nb, C, D_MODEL))

    lane_lo = lax.broadcasted_iota(jnp.int32, (1, C, LANES), 2) < RWKV_HEAD
    er = lax.broadcasted_iota(jnp.int32, (LANES, LANES), 0)
    ec = lax.broadcasted_iota(jnp.int32, (LANES, LANES), 1)
    seg_ones = ((er // RWKV_HEAD) == (ec // RWKV_HEAD)).astype(BF16)
    strict = ((er % C) > (ec % C))[None]
    incl = ((er % C) >= (ec % C))[None]

    def seg(x):
        hi, lo = _split2(x.reshape(G * C, LANES))
        return (_dot(hi, seg_ones) + _dot(lo, seg_ones)).reshape(G, C, LANES)

    kk = k * per_pair(kk_ref)
    kk = kk / jnp.maximum(jnp.sqrt(seg(kk * kk)), 1e-12)
    k2 = k * (1.0 + (a - 1.0) * per_pair(ka_ref))
    bonus = seg(r * k2 * per_pair(rk_ref))
    at = -kk * e_prev
    rt = r * e_cur
    bt = kk * a * e_neg
    kt = k2 * e_neg
    st = s_scr[...]
    sp = (st * g_ref).astype(BF16)

    lhs = jnp.concatenate([_stack_heads(at, lane_lo), _stack_heads(rt, lane_lo)], axis=1).astype(BF16)
    rhs = jnp.concatenate([_stack_heads(bt, lane_lo), _stack_heads(kt, lane_lo)], axis=1).astype(BF16)
    m1 = _bmm_nt(lhs, rhs)
    n2 = 2 * C
    ab = jnp.where(strict, m1[:, :n2, :n2], 0.0).astype(BF16)
    ak = jnp.where(strict, m1[:, :n2, n2:], 0.0).astype(BF16)
    rb = jnp.where(incl, m1[:, n2:, :n2], 0.0).astype(BF16)
    rk = jnp.where(incl, m1[:, n2:, n2:], 0.0).astype(BF16)
    vs = _stack_heads(v, lane_lo).astype(BF16)

    pm = _bmm_nt(lhs[:, :n2], sp) + _bmm(ak, vs)
    x = ab
    for i in range(6):
        pm = pm + _bmm(x, pm.astype(BF16))
        if i < 5:
            x = _bmm(x, x).astype(BF16)
    pmb = pm.astype(BF16)
    y2 = _bmm_nt(lhs[:, n2:], sp) + _bmm(rb, pmb) + _bmm(rk, vs)
    y = y2[:, :C] + y2[:, C:]

    upd_l = jnp.concatenate([pmb, vs], axis=1)
    upd_r = jnp.concatenate([_stack_heads(bt * g_last, lane_lo),
                             _stack_heads(kt * g_last, lane_lo)], axis=1).astype(BF16)
    s_scr[...] = st * g_c + _bmm_tn(upd_l, upd_r)

    mean = seg(y) * (1.0 / RWKV_HEAD)
    yc = y - mean
    var = seg(yc * yc) * (1.0 / RWKV_HEAD)
    y = yc * lax.rsqrt(var + RWKV_GN_EPS) * per_pair(lng_ref) + per_pair(lnb_ref) + bonus * v

    y = _merge_lanes(y, RWKV_PAIRS)
    orw = _dot((y * g).astype(BF16), wo_ref[...])
    mix = mg_ref[...].reshape(rows, D_MODEL) + jax.nn.sigmoid(gate_ref[...].reshape(rows, D_MODEL)) * orw
    out = x_ref[...].reshape(rows, D_MODEL) + _dot(mix.astype(BF16), wout_ref[...])
    o_ref[...] = out.reshape(nb, C, D_MODEL)


def _rwkv(p3, mg, x, prm, nb):
    batch, lp, _ = p3.shape
    n_chunks = lp // CHUNK
    real = lambda b, c: (b, jnp.maximum(c - 1, 0), 0)
    pspec = lambda w, col: pl.BlockSpec((nb, CHUNK, w), lambda b, c: (b, c, col // w))
    full = lambda shape: pl.BlockSpec(shape, lambda b, c: (0,) * len(shape))
    pairvec = full((RWKV_PAIRS, 1, LANES))
    vec = full((1, D_MODEL))
    shift = lambda w: pltpu.VMEM((nb, CHUNK + SUBLANES, w), F32)
    return pl.pallas_call(
        _rwkv_body,
        grid=(batch // nb, n_chunks),
        in_specs=[
            pspec(D_MODEL, COL_RR), pspec(D_MODEL, COL_RK), pspec(D_MODEL, COL_RV),
            pspec(SMALL_W, COL_SMALL), pspec(D_MODEL, COL_GATE_RWKV),
            pl.BlockSpec((nb, CHUNK, D_MODEL), real), pl.BlockSpec((nb, CHUNK, D_MODEL), real),
            vec, vec, vec, full((1, SMALL_W)),
            full((LANES, D_MODEL)), vec, full((LANES, D_MODEL)), vec, full((2 * LANES, D_MODEL)),
            pairvec, pairvec, pairvec, pairvec, pairvec,
            full((D_MODEL, D_MODEL)), full((D_MODEL, D_MODEL)),
        ],
        out_specs=pl.BlockSpec((nb, CHUNK, D_MODEL), real),
        out_shape=jax.ShapeDtypeStruct(x.shape, F32),
        scratch_shapes=[shift(D_MODEL), shift(D_MODEL), shift(D_MODEL), shift(SMALL_W),
                        pltpu.VMEM((nb * RWKV_PAIRS, LANES, LANES), F32)],
        compiler_params=pltpu.CompilerParams(
            dimension_semantics=("parallel", "arbitrary"), vmem_limit_bytes=VMEM_LIMIT),
        name="rwkv",
    )(p3, p3, p3, p3, p3, mg, x, *prm)


def _route_body(h_ref, g_ref, wr_ref, br_ref, idx_ref, prob_ref, rank_ref, cnt_ref, base_scr):
    tm = h_ref.shape[0]

    @pl.when(pl.program_id(0) == 0)
    def _():
        base_scr[...] = jnp.zeros_like(base_scr)

    h = h_ref[...]
    u = h * lax.rsqrt(jnp.mean(h * h, axis=-1, keepdims=True) + NORM_EPS) * g_ref[...]
    hi, mid, lo = _split3(u)
    whi, wmid, wlo = wr_ref[0], wr_ref[1], wr_ref[2]
    logits = (_dot(hi, whi) + (_dot(hi, wmid) + _dot(mid, whi))
              + (_dot(hi, wlo) + _dot(mid, wmid) + _dot(lo, whi))) + br_ref[...]
    lane = lax.broadcasted_iota(jnp.int32, (tm, LANES), 1)
    logits = jnp.where(lane < N_EXPERTS, logits, -jnp.inf)
    idx_out = jnp.zeros((tm, LANES), jnp.int32)
    prob_out = jnp.zeros((tm, LANES), F32)
    denom = jnp.zeros((tm, 1), F32)
    hits = []
    top = None
    for k in range(TOP_K):
        m = jnp.max(logits, axis=-1, keepdims=True)
        idx = jnp.min(jnp.where(logits == m, lane, LANES), axis=-1, keepdims=True)
        hit = lane == idx
        top = m if top is None else top
        w = jnp.exp(m - top)
        idx_out = jnp.where(lane == k, idx, idx_out)
        prob_out = jnp.where(lane == k, w, prob_out)
        denom = denom + w
        hits.append(hit)
        logits = jnp.where(hit, -jnp.inf, logits)
    idx_ref[...] = idx_out
    prob_ref[...] = prob_out / denom

    tot = jnp.zeros((tm, LANES), F32)
    for hit in hits:
        tot = tot + jnp.where(hit, 1.0, 0.0)
    r = lax.broadcasted_iota(jnp.int32, (tm, tm), 0)
    c = lax.broadcasted_iota(jnp.int32, (tm, tm), 1)
    before = _dot((r > c).astype(BF16), tot.astype(BF16)) + base_scr[...]
    rank_out = jnp.zeros((tm, LANES), F32)
    for k, hit in enumerate(hits):
        rk = jnp.sum(jnp.where(hit, before, 0.0), axis=-1, keepdims=True)
        rank_out = jnp.where(lane == k, rk, rank_out)
    rank_ref[...] = rank_out.astype(jnp.int32)
    base_scr[...] += jnp.sum(tot, axis=0, keepdims=True)
    cnt_ref[...] = base_scr[...]


def _route(h2, g, wr3, br, tm):
    t = h2.shape[0]
    full = lambda shape: pl.BlockSpec(shape, lambda i: (0,) * len(shape))
    tile = lambda w: pl.BlockSpec((tm, w), lambda i: (i, 0))
    return pl.pallas_call(
        _route_body,
        grid=(t // tm,),
        in_specs=[tile(D_MODEL), full((1, D_MODEL)), full((3, D_MODEL, LANES)), full((1, LANES))],
        out_specs=[tile(LANES), tile(LANES), tile(LANES), full((1, LANES))],
        out_shape=[jax.ShapeDtypeStruct((t, LANES), jnp.int32),
                   jax.ShapeDtypeStruct((t, LANES), F32),
                   jax.ShapeDtypeStruct((t, LANES), jnp.int32),
                   jax.ShapeDtypeStruct((1, LANES), F32)],
        scratch_shapes=[pltpu.VMEM((1, LANES), F32)],
        compiler_params=pltpu.CompilerParams(
            dimension_semantics=("arbitrary",), vmem_limit_bytes=VMEM_LIMIT),
        name="route",
    )(h2, g, wr3, br)


def _dispatch_body(tail_ref, dest_ref, h_ref, xs_ref, hs_scr, sem):
    tm = h_ref.shape[0]
    zrows = hs_scr.shape[0]

    @pl.when(pl.program_id(0) == 0)
    def _():
        hs_scr[...] = jnp.zeros_like(hs_scr)
        for e in range(N_EXPERTS):
            start = pl.multiple_of(tail_ref[e] * SUBLANES, SUBLANES)
            pltpu.make_async_copy(hs_scr, xs_ref.at[pl.ds(start, zrows)], sem).start()
        for e in range(N_EXPERTS):
            pltpu.make_async_copy(hs_scr, xs_ref.at[pl.ds(0, zrows)], sem).wait()

        def zero_tile(j, carry):
            start = pl.multiple_of(j * zrows, zrows)
            pltpu.make_async_copy(hs_scr, xs_ref.at[pl.ds(start, zrows)], sem).start()
            return carry

        def wait_tile(j, carry):
            pltpu.make_async_copy(hs_scr, xs_ref.at[pl.ds(0, zrows)], sem).wait()
            return carry

        n_tiles = xs_ref.shape[0] // zrows
        lax.fori_loop(tail_ref[N_EXPERTS], n_tiles, zero_tile, 0)
        lax.fori_loop(tail_ref[N_EXPERTS], n_tiles, wait_tile, 0)

    for c in range(SUBLANES):
        hs_scr[pl.ds(c, tm, stride=SUBLANES), :] = h_ref[:, c * LANES:(c + 1) * LANES]

    def issue(t, carry):
        src = hs_scr.at[pl.ds(pl.multiple_of(t * SUBLANES, SUBLANES), SUBLANES)]
        for k in range(TOP_K):
            d = pl.multiple_of(dest_ref[0, t * TOP_K + k] * SUBLANES, SUBLANES)
            pltpu.make_async_copy(src, xs_ref.at[pl.ds(d, SUBLANES)], sem).start(priority=k % 2)
        return carry

    lax.fori_loop(0, tm, issue, 0, unroll=DMA_UNROLL)
    for k in range(TOP_K):
        pltpu.make_async_copy(hs_scr, xs_ref.at[pl.ds(0, tm * SUBLANES)], sem).wait()


def _dispatch(tail, dest2, h2, ns, tm):
    t = h2.shape[0]
    return pl.pallas_call(
        _dispatch_body,
        grid_spec=pltpu.PrefetchScalarGridSpec(
            num_scalar_prefetch=1,
            grid=(t // tm,),
            in_specs=[pl.BlockSpec((None, 1, tm * TOP_K), lambda i, tl: (i, 0, 0), memory_space=pltpu.SMEM),
                      pl.BlockSpec((tm, D_MODEL), lambda i, tl: (i, 0))],
            out_specs=pl.BlockSpec(memory_space=pl.ANY),
            scratch_shapes=[pltpu.VMEM((tm * SUBLANES, LANES), F32), pltpu.SemaphoreType.DMA(())],
        ),
        out_shape=jax.ShapeDtypeStruct((ns * SUBLANES, LANES), F32),
        compiler_params=pltpu.CompilerParams(
            dimension_semantics=("arbitrary",), vmem_limit_bytes=VMEM_LIMIT),
        name="dispatch",
    )(tail, dest2, h2)


def _experts_body(te_ref, nu_ref, xs_ref, g_ref, wgu_ref, bgu_ref, wdn_ref, bdn_ref, ys_ref,
                  wgu_scr, wdn_scr):
    i = pl.program_id(0)
    tm = xs_ref.shape[0] // SUBLANES

    @pl.when((i == 0) | (te_ref[i] != te_ref[jnp.maximum(i - 1, 0)]))
    def _():
        wgu_scr[...] = wgu_ref[0].astype(BF16)
        wdn_scr[...] = wdn_ref[0].astype(BF16)

    @pl.when(i < nu_ref[0])
    def _():
        x = jnp.concatenate([xs_ref[pl.ds(c, tm, stride=SUBLANES), :] for c in range(SUBLANES)], axis=1)
        u = x * lax.rsqrt(jnp.mean(x * x, axis=-1, keepdims=True) + NORM_EPS) * g_ref[...]
        hgu = _dot(u.astype(BF16), wgu_scr[...]) + bgu_ref[0]
        d = wdn_scr.shape[0]
        gate = jnp.minimum(hgu[:, :d], SWIGLU_LIMIT)
        up = jnp.clip(hgu[:, d:], -SWIGLU_LIMIT, SWIGLU_LIMIT)
        act = (up + 1.0) * (gate * jax.nn.sigmoid(gate * SWIGLU_ALPHA))
        y = _dot(act.astype(BF16), wdn_scr[...]) + bdn_ref[0]
        for c in range(SUBLANES):
            ys_ref[pl.ds(c, tm, stride=SUBLANES), :] = y[:, c * LANES:(c + 1) * LANES]

    @pl.when(i >= nu_ref[0])
    def _():
        ys_ref[...] = jnp.zeros_like(ys_ref)


def _experts(tile_expert, n_used, xs, g, wgu, bgu, wdn, bdn, tm):
    n_tiles = xs.shape[0] // (tm * SUBLANES)
    wsel = lambda shape: pl.BlockSpec(shape, lambda i, te, nu: (te[i], 0, 0))
    used = lambda i, te, nu: (jnp.minimum(i, nu[0] - 1), 0)
    return pl.pallas_call(
        _experts_body,
        grid_spec=pltpu.PrefetchScalarGridSpec(
            num_scalar_prefetch=2,
            grid=(n_tiles,),
            in_specs=[pl.BlockSpec((tm * SUBLANES, LANES), used),
                      pl.BlockSpec((1, D_MODEL), lambda i, te, nu: (0, 0)),
                      wsel((1, D_MODEL, 2 * D_MODEL)), wsel((1, 1, 2 * D_MODEL)),
                      wsel((1, D_MODEL, D_MODEL)), wsel((1, 1, D_MODEL))],
            out_specs=pl.BlockSpec((tm * SUBLANES, LANES), lambda i, te, nu: (i, 0)),
            scratch_shapes=[pltpu.VMEM((D_MODEL, 2 * D_MODEL), BF16), pltpu.VMEM((D_MODEL, D_MODEL), BF16)],
        ),
        out_shape=jax.ShapeDtypeStruct(xs.shape, F32),
        compiler_params=pltpu.CompilerParams(
            dimension_semantics=("arbitrary",), vmem_limit_bytes=VMEM_LIMIT),
        name="experts",
    )(tile_expert, n_used, xs, g, wgu, bgu, wdn, bdn)


def _combine_body(dcur_ref, dnext_ref, prob_ref, h_ref, gf_ref, ys_ref, o_ref, ybuf, sem):
    i = pl.program_id(0)
    n = pl.num_programs(0)
    tm = h_ref.shape[0]
    slot = i % 2

    def gather(dref, s):
        def issue(t, carry):
            row = pl.multiple_of(t * SUBLANES, SUBLANES)
            for k in range(TOP_K):
                d = pl.multiple_of(dref[0, t * TOP_K + k] * SUBLANES, SUBLANES)
                pltpu.make_async_copy(ys_ref.at[pl.ds(d, SUBLANES)], ybuf.at[s, k, pl.ds(row, SUBLANES)],
                                      sem.at[s]).start(priority=k % 2)
            return carry
        lax.fori_loop(0, tm, issue, 0, unroll=DMA_UNROLL)

    @pl.when(i == 0)
    def _():
        gather(dcur_ref, 0)

    @pl.when(i + 1 < n)
    def _():
        gather(dnext_ref, 1 - slot)

    for k in range(TOP_K):
        pltpu.make_async_copy(ys_ref.at[pl.ds(0, tm * SUBLANES)], ybuf.at[slot, k], sem.at[slot]).wait()
    prob = prob_ref[...]
    cols = []
    for c in range(SUBLANES):
        acc = h_ref[:, c * LANES:(c + 1) * LANES]
        for k in range(TOP_K):
            acc = acc + prob[:, k:k + 1] * ybuf[slot, k, pl.ds(c, tm, stride=SUBLANES), :]
        cols.append(acc)
    h = jnp.concatenate(cols, axis=1)
    o_ref[...] = h * lax.rsqrt(jnp.mean(h * h, axis=-1, keepdims=True) + NORM_EPS) * gf_ref[...]


def _combine(dest2, prob, h2, gf, ys, tm):
    t = h2.shape[0]
    n = t // tm
    smem = lambda imap: pl.BlockSpec((None, 1, tm * TOP_K), imap, memory_space=pltpu.SMEM)
    return pl.pallas_call(
        _combine_body,
        grid=(n,),
        in_specs=[smem(lambda i: (i, 0, 0)), smem(lambda i: (jnp.minimum(i + 1, n - 1), 0, 0)),
                  pl.BlockSpec((tm, LANES), lambda i: (i, 0)),
                  pl.BlockSpec((tm, D_MODEL), lambda i: (i, 0)),
                  pl.BlockSpec((1, D_MODEL), lambda i: (0, 0)),
                  pl.BlockSpec(memory_space=pl.ANY)],
        out_specs=pl.BlockSpec((tm, D_MODEL), lambda i: (i, 0)),
        out_shape=jax.ShapeDtypeStruct((t, D_MODEL), F32),
        scratch_shapes=[pltpu.VMEM((2, TOP_K, tm * SUBLANES, LANES), F32), pltpu.SemaphoreType.DMA((2,))],
        compiler_params=pltpu.CompilerParams(
            dimension_semantics=("arbitrary",), vmem_limit_bytes=VMEM_LIMIT),
        name="combine",
    )(dest2, dest2, prob, h2, gf, ys)


def _moe(h2, g, wr3, br, wgu, bgu, wdn, bdn, gf):
    t = h2.shape[0]
    tm_x = _pick_tile(t, (512, 256, 128))
    tm_c = _pick_tile(t, (256, 128))
    idx, prob, rank, cnt = _route(h2, g, wr3, br, tm_x)

    counts = cnt[0, :N_EXPERTS].astype(jnp.int32)
    padded = (counts + tm_x - 1) // tm_x * tm_x
    ends = jnp.cumsum(padded)
    offs = ends - padded
    ns = t * TOP_K + (N_EXPERTS + 1) * tm_x
    e_ids = jnp.arange(N_EXPERTS, dtype=jnp.int32)
    idx4 = idx[:, :TOP_K]
    dest = rank[:, :TOP_K] + jnp.sum(jnp.where(idx4[..., None] == e_ids, offs, 0), axis=-1)
    tile_start = jnp.arange(ns // tm_x, dtype=jnp.int32) * tm_x
    tile_expert = jnp.minimum(jnp.sum((tile_start[:, None] >= ends[None, :]).astype(jnp.int32), axis=1),
                              N_EXPERTS - 1)
    n_used = (ends[-1] // tm_x).reshape(1)

    xs = _dispatch(jnp.concatenate([offs + counts, n_used]), dest.reshape(t // tm_x, 1, tm_x * TOP_K), h2, ns, tm_x)
    ys = _experts(tile_expert, n_used, xs, g, wgu, bgu, wdn, bdn, tm_x)
    return _combine(dest.reshape(t // tm_c, 1, tm_c * TOP_K), prob, h2, gf, ys, tm_c)


def _pick_tile(n, prefs):
    for t in prefs:
        if n % t == 0:
            return t
    return n


def _pad_rows(w, rows, offset=0):
    out = jnp.zeros((rows, w.shape[1]), w.dtype)
    return out.at[offset:offset + w.shape[0]].set(w)


def kernel(x, meta_tokens, norm_mix_g, w_in, w_gla_a2, b_gla_a, gla_norm_g, w_gla_o, mu_r, mu_k, mu_v, mu_w, mu_a, mu_g, w_decay2, b_decay, w_a2, b_a, w_gate2, k_k, k_a, r_k, ln_x_g, ln_x_b, w_rwkv_o, w_out, norm_ffn_g, w_router, b_router, w_exp_gu, b_exp_gu, w_exp_down, b_exp_down, norm_final_g):
    batch, seq, d = x.shape
    assert d == D_MODEL and seq % CHUNK == 0 and w_in.shape[0] == 1
    lp = FRONT_PAD + N_META + seq
    row = lambda a: a.reshape(1, -1)

    meta = jnp.broadcast_to(meta_tokens[None].astype(x.dtype), (batch, N_META, d))
    hp = jnp.concatenate([jnp.zeros((batch, FRONT_PAD, d), x.dtype), meta, x], axis=1).reshape(batch * lp, d)

    splits = (GLA_KEY, GLA_KEY, D_MODEL, D_MODEL, GLA_GATE_RANK, D_MODEL, D_MODEL, D_MODEL,
              RWKV_DECAY_RANK, RWKV_A_RANK, RWKV_GATE_RANK, D_MODEL, D_MODEL)
    offs = [0]
    for s in splits:
        offs.append(offs[-1] + s)
    piece = lambda i: w_in[0][:, offs[i]:offs[i + 1]]
    zcols = lambda n: jnp.zeros((d, n), w_in.dtype)
    small = jnp.concatenate([piece(4), zcols(LANES - GLA_GATE_RANK), piece(8), piece(9), piece(10),
                             zcols(2 * LANES - RWKV_GATE_RANK)], axis=1)
    w_big = jnp.concatenate([piece(0), piece(1), piece(2), piece(3), piece(5), piece(6), piece(7),
                             piece(11), piece(12), small], axis=1).astype(BF16)

    m = batch * lp
    tm = _pick_tile(m, (1280, 640, 512, 320, 256, 192, 128, 64))
    tn = _pick_tile(NP_COLS, (2176, 512))
    p = _inproj(hp, row(norm_mix_g[0]), w_big, tm, tn)

    p3 = p.reshape(batch, lp, NP_COLS)
    nb = _pick_tile(batch, (SEQS_PER_STEP, 1))
    mg = _gla(p3, _pad_rows(w_gla_a2[0], LANES).astype(BF16), row(b_gla_a[0]), row(gla_norm_g[0]),
              w_gla_o[0].astype(BF16), nb)

    pv = lambda a: a.reshape(RWKV_PAIRS, 1, LANES)
    mu_small = jnp.concatenate([jnp.zeros((LANES,), F32), mu_w[0], mu_a[0], mu_g[0],
                                jnp.zeros((2 * LANES - RWKV_GATE_RANK,), F32)])
    prm = (row(mu_r[0]), row(mu_k[0]), row(mu_v[0]), row(mu_small),
           _pad_rows(w_decay2[0], LANES).astype(BF16), row(b_decay[0]),
           _pad_rows(w_a2[0], LANES, RWKV_DECAY_RANK).astype(BF16), row(b_a[0]),
           _pad_rows(w_gate2[0], 2 * LANES).astype(BF16),
           pv(k_k[0]), pv(k_a[0]), pv(r_k[0]), pv(ln_x_g[0]), pv(ln_x_b[0]),
           w_rwkv_o[0].astype(BF16), w_out[0].astype(BF16))
    h2 = _rwkv(p3, mg, x, prm, nb).reshape(batch * seq, d)

    wr = jnp.zeros((d, LANES), F32).at[:, :N_EXPERTS].set(w_router[0])
    wr3 = jnp.stack(_split3(wr))
    br = jnp.zeros((1, LANES), F32).at[0, :N_EXPERTS].set(b_router[0])
    out = _moe(h2, row(norm_ffn_g[0]), wr3, br, w_exp_gu[0], b_exp_gu[0][:, None, :],
               w_exp_down[0], b_exp_down[0][:, None, :], row(norm_final_g))
    return out.reshape(batch, seq, d)
```

```python
import math

import jax
import jax.numpy as jnp
from jax import lax
from jax.experimental import pallas as pl
from jax.experimental.pallas import tpu as pltpu

F32 = jnp.float32
BF16 = jnp.bfloat16

D_MODEL = 1024
N_META = 16
NORM_EPS = 1e-5
CHUNK = 64
FRONT_PAD = (-N_META) % CHUNK
GLA_HEADS = 4
GLA_DK = 128
GLA_DV = 256
GLA_KEY = GLA_HEADS * GLA_DK
GLA_GATE_RANK = 16
GLA_GATE_NORM = 16.0
RWKV_HEAD = 64
RWKV_PAIRS = D_MODEL // (2 * RWKV_HEAD)
RWKV_DECAY_RANK = 64
RWKV_A_RANK = 64
RWKV_GATE_RANK = 160
RWKV_GN_EPS = 64e-5
RWKV_DECAY_SCALE = math.exp(-0.5)
N_EXPERTS = 32
TOP_K = 4
SWIGLU_LIMIT = 7.0
SWIGLU_ALPHA = 1.702
SEQS_PER_STEP = 2
DMA_UNROLL = 8
LANES = 128
SUBLANES = 8
SMALL_W = 512
COL_GQ, COL_GK, COL_GV, COL_GR = 0, 512, 1024, 2048
COL_RR, COL_RK, COL_RV = 3072, 4096, 5120
COL_GATE_GLA, COL_GATE_RWKV, COL_SMALL = 6144, 7168, 8192
NP_COLS = COL_SMALL + SMALL_W
VMEM_LIMIT = 56 * 1024 * 1024


def _dot(a, b):
    return jnp.dot(a, b, preferred_element_type=F32)


def _bmm(a, b):
    return jnp.einsum("gik,gkj->gij", a, b, preferred_element_type=F32)


def _bmm_nt(a, b):
    return jnp.einsum("gik,gjk->gij", a, b, preferred_element_type=F32)


def _bmm_tn(a, b):
    return jnp.einsum("gti,gtj->gij", a, b, preferred_element_type=F32)


def _split2(x):
    hi = x.astype(BF16)
    lo = (x - hi.astype(F32)).astype(BF16)
    return hi, lo


def _split3(x):
    hi = x.astype(BF16)
    r1 = x - hi.astype(F32)
    mid = r1.astype(BF16)
    lo = (r1 - mid.astype(F32)).astype(BF16)
    return hi, mid, lo


def _cumsum_rows(x, tri, split):
    parts = split(x)
    out = _dot(tri, parts[0])
    for p in parts[1:]:
        out = out + _dot(tri, p)
    return out


def _softplus(x):
    return jnp.maximum(x, 0.0) + jnp.log1p(jnp.exp(-jnp.abs(x)))


def _tri_blocks(rows, blk):
    r = lax.broadcasted_iota(jnp.int32, (rows, rows), 0)
    c = lax.broadcasted_iota(jnp.int32, (rows, rows), 1)
    return ((r >= c) & (r // blk == c // blk)).astype(BF16)


def _split_lanes(x, n, w):
    s, c, _ = x.shape
    return jnp.stack([x[:, :, j * w:(j + 1) * w] for j in range(n)], axis=1).reshape(s * n, c, w)


def _merge_lanes(x, n):
    sn, c, w = x.shape
    x = x.reshape(sn // n, n, c, w)
    return jnp.concatenate([x[:, j] for j in range(n)], axis=-1).reshape(sn // n * c, n * w)


def _inproj_body(x_ref, g_ref, w_ref, o_ref, u_scr):
    @pl.when(pl.program_id(1) == 0)
    def _():
        x = x_ref[...]
        ms = jnp.mean(x * x, axis=-1, keepdims=True)
        u_scr[...] = (x * lax.rsqrt(ms + NORM_EPS) * g_ref[...]).astype(BF16)

    o_ref[...] = _dot(u_scr[...], w_ref[...])


def _inproj(hp, g, w, tm, tn):
    m = hp.shape[0]
    return pl.pallas_call(
        _inproj_body,
        grid=(m // tm, NP_COLS // tn),
        in_specs=[
            pl.BlockSpec((tm, D_MODEL), lambda i, j: (i, 0)),
            pl.BlockSpec((1, D_MODEL), lambda i, j: (0, 0)),
            pl.BlockSpec((D_MODEL, tn), lambda i, j: (0, j)),
        ],
        out_specs=pl.BlockSpec((tm, tn), lambda i, j: (i, j)),
        out_shape=jax.ShapeDtypeStruct((m, NP_COLS), F32),
        scratch_shapes=[pltpu.VMEM((tm, D_MODEL), BF16)],
        compiler_params=pltpu.CompilerParams(
            dimension_semantics=("parallel", "arbitrary"), vmem_limit_bytes=VMEM_LIMIT),
        name="inproj",
    )(hp, g, w)


def _gla_body(q_ref, k_ref, v_ref, r_ref, sm_ref, gate_ref, wa2_ref, ba_ref, ng_ref, wo_ref,
              o_ref, s_scr):
    c = pl.program_id(1)
    nb = q_ref.shape[0]
    C = CHUNK
    rows = nb * C
    H = GLA_HEADS

    @pl.when(c == 0)
    def _():
        s_scr[...] = jnp.zeros_like(s_scr)

    row = lax.broadcasted_iota(jnp.int32, (rows, 1), 0) % C
    causal = (lax.broadcasted_iota(jnp.int32, (C, C), 0) >= lax.broadcasted_iota(jnp.int32, (C, C), 1))[None]

    z = _dot(sm_ref[...].reshape(rows, SMALL_W)[:, 0:LANES].astype(BF16), wa2_ref[...]) + ba_ref[...]
    gk = -_softplus(-z) * (1.0 / GLA_GATE_NORM)
    gk = jnp.where((c > 0) | (row >= FRONT_PAD), gk, 0.0)
    b = _cumsum_rows(gk, _tri_blocks(rows, C), _split3).reshape(nb, C, GLA_KEY)
    b_ref = b[:, C // 2 - 1:C // 2, :]
    b_last = b[:, C - 1:C, :]

    q = q_ref[...] * (GLA_DK ** -0.5)
    k = k_ref[...]
    keys = lambda x: _split_lanes(x, H, GLA_DK)
    vals = lambda x: _split_lanes(x, H, GLA_DV)
    qe = keys((q * jnp.exp(b - b_ref)).astype(BF16))
    ke = keys((k * jnp.exp(b_ref - b)).astype(BF16))
    qs = keys((q * jnp.exp(b)).astype(BF16))
    kd = keys((k * jnp.exp(b_last - b)).astype(BF16))
    decay = keys(jnp.exp(b_last))
    v = vals(v_ref[...].astype(BF16))
    silu_r = r_ref[...]
    silu_r = vals(silu_r * jax.nn.sigmoid(silu_r))

    a = jnp.where(causal, _bmm_nt(qe, ke), 0.0)
    st = s_scr[...]
    o = _bmm(a.astype(BF16), v) + _bmm_nt(qs, st.astype(BF16))
    s_scr[...] = st * decay + _bmm_tn(v, kd)
    o = o * lax.rsqrt(jnp.mean(o * o, axis=-1, keepdims=True) + NORM_EPS) * ng_ref[...]
    og = _merge_lanes(o * silu_r, H).astype(BF16)
    out = jax.nn.sigmoid(gate_ref[...].reshape(rows, D_MODEL)) * _dot(og, wo_ref[...])
    o_ref[...] = out.reshape(nb, C, D_MODEL)


def _gla(p3, wa2, ba, ng, wo, nb):
    batch, lp, _ = p3.shape
    n_chunks = lp // CHUNK
    pspec = lambda w, col: pl.BlockSpec((nb, CHUNK, w), lambda b, c: (b, c, col // w))
    full = lambda shape: pl.BlockSpec(shape, lambda b, c: (0,) * len(shape))
    return pl.pallas_call(
        _gla_body,
        grid=(batch // nb, n_chunks),
        in_specs=[
            pspec(GLA_KEY, COL_GQ), pspec(GLA_KEY, COL_GK), pspec(D_MODEL, COL_GV),
            pspec(D_MODEL, COL_GR), pspec(SMALL_W, COL_SMALL), pspec(D_MODEL, COL_GATE_GLA),
            full((LANES, GLA_KEY)), full((1, GLA_KEY)), full((1, GLA_DV)), full((D_MODEL, D_MODEL)),
        ],
        out_specs=pl.BlockSpec((nb, CHUNK, D_MODEL), lambda b, c: (b, jnp.maximum(c - 1, 0), 0)),
        out_shape=jax.ShapeDtypeStruct((batch, (n_chunks - 1) * CHUNK, D_MODEL), F32),
        scratch_shapes=[pltpu.VMEM((nb * GLA_HEADS, GLA_DV, GLA_DK), F32)],
        compiler_params=pltpu.CompilerParams(
            dimension_semantics=("parallel", "arbitrary"), vmem_limit_bytes=VMEM_LIMIT),
        name="gla",
    )(p3, p3, p3, p3, p3, p3, wa2, ba, ng, wo)


def _stack_heads(x, lane_lo):
    zero = jnp.zeros((), x.dtype)
    return jnp.concatenate([jnp.where(lane_lo, x, zero), jnp.where(lane_lo, zero, x)], axis=1)


def _rwkv_body(pr_ref, pk_ref, pv_ref, sm_ref, gate_ref, mg_ref, x_ref,
               mur_ref, muk_ref, muv_ref, musm_ref, wd_ref, bd_ref, wa_ref, ba_ref, wg_ref,
               kk_ref, ka_ref, rk_ref, lng_ref, lnb_ref, wo_ref, wout_ref,
               o_ref,
               shr_scr, shk_scr, shv_scr, shs_scr, s_scr):
    c = pl.program_id(1)
    nb = pr_ref.shape[0]
    C = CHUNK
    rows = nb * C
    G = nb * RWKV_PAIRS

    @pl.when(c == 0)
    def _():
        s_scr[...] = jnp.zeros_like(s_scr)
        for scr in (shr_scr, shk_scr, shv_scr, shs_scr):
            scr[:, 7:8, :] = jnp.zeros((nb, 1, scr.shape[2]), F32)

    def lerp(x_ref_, scr, mu_ref):
        x = x_ref_[...]
        scr[:, 8:8 + C, :] = x
        prev = scr[:, 7:7 + C, :]
        scr[:, 7:8, :] = x[:, C - 1:C, :]
        return x + (prev - x) * mu_ref[...]

    pairs = lambda x: _split_lanes(x, RWKV_PAIRS, LANES)
    per_pair = lambda ref: jnp.broadcast_to(ref[...][None], (nb, RWKV_PAIRS, 1, LANES)).reshape(G, 1, LANES)

    r = lerp(pr_ref, shr_scr, mur_ref)
    k = lerp(pk_ref, shk_scr, muk_ref)
    v = lerp(pv_ref, shv_scr, muv_ref)
    ls = lerp(sm_ref, shs_scr, musm_ref).reshape(rows, SMALL_W)
    s1 = ls[:, LANES:2 * LANES]
    xw = bd_ref[...] + _dot(jnp.tanh(s1).astype(BF16), wd_ref[...])
    logw = -RWKV_DECAY_SCALE * jax.nn.sigmoid(xw)
    a = jax.nn.sigmoid(ba_ref[...] + _dot(s1.astype(BF16), wa_ref[...]))
    g = _dot(jax.nn.sigmoid(ls[:, 2 * LANES:4 * LANES]).astype(BF16), wg_ref[...])

    cl = _cumsum_rows(logw, _tri_blocks(rows, C), _split2).reshape(nb, C, D_MODEL)
    logw = logw.reshape(nb, C, D_MODEL)
    cref = cl[:, C // 2 - 1:C // 2, :]
    clast = cl[:, C - 1:C, :]
    e_neg = pairs(jnp.exp(cref - cl))
    e_prev = pairs(jnp.exp(cl - logw - cref))
    e_cur = pairs(jnp.exp(cl - cref))
    g_last = pairs(jnp.exp(clast - cref))
    g_ref = pairs(jnp.exp(cref))
    g_c = pairs(jnp.exp(clast))
    r = pairs(r)
    k = pairs(k)
    v = pairs(v)
    a = pairs(a.reshape(nb, C, D_MODEL))

    lane_lo = lax.broadcasted_iota(jnp.int32, (1, C, LANES), 2) < RWKV_HEAD
    er = lax.broadcasted_iota(jnp.int32, (LANES, LANES), 0)
    ec = lax.broadcasted_iota(jnp.int32, (LANES, LANES), 1)
    seg_ones = ((er // RWKV_HEAD) == (ec // RWKV_HEAD)).astype(BF16)
    strict = ((er % C) > (ec % C))[None]
    incl = ((er % C) >= (ec % C))[None]

    def seg(x):
        return _dot(x.reshape(G * C, LANES).astype(BF16), seg_ones).reshape(G, C, LANES)

    kk = k * per_pair(kk_ref)
    kk = kk * lax.rsqrt(jnp.maximum(seg(kk * kk), 1e-24))
    k2 = k * (1.0 + (a - 1.0) * per_pair(ka_ref))
    bonus = seg(r * k2 * per_pair(rk_ref))
    at = -kk * e_prev
    rt = r * e_cur
    bt = kk * a * e_neg
    kt = k2 * e_neg
    st = s_scr[...]
    sp = (st * g_ref).astype(BF16)

    stk = lambda z: _stack_heads(z.astype(BF16), lane_lo)
    lhs = jnp.concatenate([stk(at), stk(rt)], axis=1)
    rhs = jnp.concatenate([stk(bt), stk(kt)], axis=1)
    m1 = _bmm_nt(lhs, rhs).astype(BF16)
    n2 = 2 * C
    ab = jnp.where(strict, m1[:, :n2, :n2], 0.0)
    ak = jnp.where(strict, m1[:, :n2, n2:], 0.0)
    rb = jnp.where(incl, m1[:, n2:, :n2], 0.0)
    rk = jnp.where(incl, m1[:, n2:, n2:], 0.0)
    vs = stk(v)

    pm = _bmm_nt(lhs[:, :n2], sp) + _bmm(ak, vs)
    x = ab
    for i in range(6):
        pm = pm + _bmm(x, pm.astype(BF16))
        if i < 5:
            x = _bmm(x, x).astype(BF16)
    pmb = pm.astype(BF16)
    y2 = _bmm_nt(lhs[:, n2:], sp) + _bmm(rb, pmb) + _bmm(rk, vs)
    y = y2[:, :C] + y2[:, C:]

    upd_l = jnp.concatenate([pmb, vs], axis=1)
    upd_r = jnp.concatenate([stk(bt * g_last), stk(kt * g_last)], axis=1)
    s_scr[...] = st * g_c + _bmm_tn(upd_l, upd_r)

    mean = seg(y) * (1.0 / RWKV_HEAD)
    yc = y - mean
    var = seg(yc * yc) * (1.0 / RWKV_HEAD)
    y = yc * lax.rsqrt(var + RWKV_GN_EPS) * per_pair(lng_ref) + per_pair(lnb_ref) + bonus * v

    y = _merge_lanes(y, RWKV_PAIRS)
    orw = _dot((y * g).astype(BF16), wo_ref[...])
    mix = mg_ref[...].reshape(rows, D_MODEL) + jax.nn.sigmoid(gate_ref[...].reshape(rows, D_MODEL)) * orw
    out = x_ref[...].reshape(rows, D_MODEL) + _dot(mix.astype(BF16), wout_ref[...])
    o_ref[...] = out.reshape(nb, C, D_MODEL)


def _rwkv(p3, mg, x, prm, nb):
    batch, lp, _ = p3.shape
    n_chunks = lp // CHUNK
    real = lambda b, c: (b, jnp.maximum(c - 1, 0), 0)
    pspec = lambda w, col: pl.BlockSpec((nb, CHUNK, w), lambda b, c: (b, c, col // w))
    full = lambda shape: pl.BlockSpec(shape, lambda b, c: (0,) * len(shape))
    pairvec = full((RWKV_PAIRS, 1, LANES))
    vec = full((1, D_MODEL))
    shift = lambda w: pltpu.VMEM((nb, CHUNK + SUBLANES, w), F32)
    return pl.pallas_call(
        _rwkv_body,
        grid=(batch // nb, n_chunks),
        in_specs=[
            pspec(D_MODEL, COL_RR), pspec(D_MODEL, COL_RK), pspec(D_MODEL, COL_RV),
            pspec(SMALL_W, COL_SMALL), pspec(D_MODEL, COL_GATE_RWKV),
            pl.BlockSpec((nb, CHUNK, D_MODEL), real), pl.BlockSpec((nb, CHUNK, D_MODEL), real),
            vec, vec, vec, full((1, SMALL_W)),
            full((LANES, D_MODEL)), vec, full((LANES, D_MODEL)), vec, full((2 * LANES, D_MODEL)),
            pairvec, pairvec, pairvec, pairvec, pairvec,
            full((D_MODEL, D_MODEL)), full((D_MODEL, D_MODEL)),
        ],
        out_specs=pl.BlockSpec((nb, CHUNK, D_MODEL), real),
        out_shape=jax.ShapeDtypeStruct(x.shape, F32),
        scratch_shapes=[shift(D_MODEL), shift(D_MODEL), shift(D_MODEL), shift(SMALL_W),
                        pltpu.VMEM((nb * RWKV_PAIRS, LANES, LANES), F32)],
        compiler_params=pltpu.CompilerParams(
            dimension_semantics=("parallel", "arbitrary"), vmem_limit_bytes=VMEM_LIMIT),
        name="rwkv",
    )(p3, p3, p3, p3, p3, mg, x, *prm)


def _route_body(h_ref, g_ref, wr_ref, br_ref, idx_ref, prob_ref, rank_ref, cnt_ref, base_scr):
    tm = h_ref.shape[0]

    @pl.when(pl.program_id(0) == 0)
    def _():
        base_scr[...] = jnp.zeros_like(base_scr)

    h = h_ref[...]
    u = h * lax.rsqrt(jnp.mean(h * h, axis=-1, keepdims=True) + NORM_EPS) * g_ref[...]
    hi, mid, lo = _split3(u)
    whi, wmid, wlo = wr_ref[0], wr_ref[1], wr_ref[2]
    logits = (_dot(hi, whi) + (_dot(hi, wmid) + _dot(mid, whi))
              + (_dot(hi, wlo) + _dot(mid, wmid) + _dot(lo, whi))) + br_ref[...]
    lane = lax.broadcasted_iota(jnp.int32, (tm, LANES), 1)
    logits = jnp.where(lane < N_EXPERTS, logits, -jnp.inf)
    idx_out = jnp.zeros((tm, LANES), jnp.int32)
    prob_out = jnp.zeros((tm, LANES), F32)
    denom = jnp.zeros((tm, 1), F32)
    hits = []
    top = None
    for k in range(TOP_K):
        m = jnp.max(logits, axis=-1, keepdims=True)
        idx = jnp.min(jnp.where(logits == m, lane, LANES), axis=-1, keepdims=True)
        hit = lane == idx
        top = m if top is None else top
        w = jnp.exp(m - top)
        idx_out = jnp.where(lane == k, idx, idx_out)
        prob_out = jnp.where(lane == k, w, prob_out)
        denom = denom + w
        hits.append(hit)
        logits = jnp.where(hit, -jnp.inf, logits)
    idx_ref[...] = idx_out
    prob_ref[...] = prob_out / denom

    tot = jnp.zeros((tm, LANES), F32)
    for hit in hits:
        tot = tot + jnp.where(hit, 1.0, 0.0)
    r = lax.broadcasted_iota(jnp.int32, (tm, tm), 0)
    c = lax.broadcasted_iota(jnp.int32, (tm, tm), 1)
    before = _dot((r > c).astype(BF16), tot.astype(BF16)) + base_scr[...]
    rank_out = jnp.zeros((tm, LANES), F32)
    for k, hit in enumerate(hits):
        rk = jnp.sum(jnp.where(hit, before, 0.0), axis=-1, keepdims=True)
        rank_out = jnp.where(lane == k, rk, rank_out)
    rank_ref[...] = rank_out.astype(jnp.int32)
    base_scr[...] += jnp.sum(tot, axis=0, keepdims=True)
    cnt_ref[...] = base_scr[...]


def _route(h2, g, wr3, br, tm):
    t = h2.shape[0]
    full = lambda shape: pl.BlockSpec(shape, lambda i: (0,) * len(shape))
    tile = lambda w: pl.BlockSpec((tm, w), lambda i: (i, 0))
    return pl.pallas_call(
        _route_body,
        grid=(t // tm,),
        in_specs=[tile(D_MODEL), full((1, D_MODEL)), full((3, D_MODEL, LANES)), full((1, LANES))],
        out_specs=[tile(LANES), tile(LANES), tile(LANES), full((1, LANES))],
        out_shape=[jax.ShapeDtypeStruct((t, LANES), jnp.int32),
                   jax.ShapeDtypeStruct((t, LANES), F32),
                   jax.ShapeDtypeStruct((t, LANES), jnp.int32),
                   jax.ShapeDtypeStruct((1, LANES), F32)],
        scratch_shapes=[pltpu.VMEM((1, LANES), F32)],
        compiler_params=pltpu.CompilerParams(
            dimension_semantics=("arbitrary",), vmem_limit_bytes=VMEM_LIMIT),
        name="route",
    )(h2, g, wr3, br)


def _dispatch_body(tail_ref, dest_ref, h_ref, xs_ref, hs_scr, sem):
    tm = h_ref.shape[0]
    zrows = hs_scr.shape[0]

    @pl.when(pl.program_id(0) == 0)
    def _():
        hs_scr[...] = jnp.zeros_like(hs_scr)
        for e in range(N_EXPERTS):
            start = pl.multiple_of(tail_ref[e] * SUBLANES, SUBLANES)
            pltpu.make_async_copy(hs_scr, xs_ref.at[pl.ds(start, zrows)], sem).start()
        for e in range(N_EXPERTS):
            pltpu.make_async_copy(hs_scr, xs_ref.at[pl.ds(0, zrows)], sem).wait()

        def zero_tile(j, carry):
            start = pl.multiple_of(j * zrows, zrows)
            pltpu.make_async_copy(hs_scr, xs_ref.at[pl.ds(start, zrows)], sem).start()
            return carry

        def wait_tile(j, carry):
            pltpu.make_async_copy(hs_scr, xs_ref.at[pl.ds(0, zrows)], sem).wait()
            return carry

        n_tiles = xs_ref.shape[0] // zrows
        lax.fori_loop(tail_ref[N_EXPERTS], n_tiles, zero_tile, 0)
        lax.fori_loop(tail_ref[N_EXPERTS], n_tiles, wait_tile, 0)

    for c in range(SUBLANES):
        hs_scr[pl.ds(c, tm, stride=SUBLANES), :] = h_ref[:, c * LANES:(c + 1) * LANES]

    def issue(t, carry):
        src = hs_scr.at[pl.ds(pl.multiple_of(t * SUBLANES, SUBLANES), SUBLANES)]
        for k in range(TOP_K):
            d = pl.multiple_of(dest_ref[0, t * TOP_K + k] * SUBLANES, SUBLANES)
            pltpu.make_async_copy(src, xs_ref.at[pl.ds(d, SUBLANES)], sem).start(priority=k % 2)
        return carry

    lax.fori_loop(0, tm, issue, 0, unroll=DMA_UNROLL)
    for k in range(TOP_K):
        pltpu.make_async_copy(hs_scr, xs_ref.at[pl.ds(0, tm * SUBLANES)], sem).wait()


def _dispatch(tail, dest2, h2, ns, tm):
    t = h2.shape[0]
    return pl.pallas_call(
        _dispatch_body,
        grid_spec=pltpu.PrefetchScalarGridSpec(
            num_scalar_prefetch=1,
            grid=(t // tm,),
            in_specs=[pl.BlockSpec((None, 1, tm * TOP_K), lambda i, tl: (i, 0, 0), memory_space=pltpu.SMEM),
                      pl.BlockSpec((tm, D_MODEL), lambda i, tl: (i, 0))],
            out_specs=pl.BlockSpec(memory_space=pl.ANY),
            scratch_shapes=[pltpu.VMEM((tm * SUBLANES, LANES), F32), pltpu.SemaphoreType.DMA(())],
        ),
        out_shape=jax.ShapeDtypeStruct((ns * SUBLANES, LANES), F32),
        compiler_params=pltpu.CompilerParams(
            dimension_semantics=("arbitrary",), vmem_limit_bytes=VMEM_LIMIT),
        name="dispatch",
    )(tail, dest2, h2)


def _experts_body(te_ref, nu_ref, xs_ref, g_ref, wgu_ref, bgu_ref, wdn_ref, bdn_ref, ys_ref,
                  wgu_scr, wdn_scr):
    i = pl.program_id(0)
    tm = xs_ref.shape[0] // SUBLANES

    @pl.when((i == 0) | (te_ref[i] != te_ref[jnp.maximum(i - 1, 0)]))
    def _():
        wgu_scr[...] = wgu_ref[0].astype(BF16)
        wdn_scr[...] = wdn_ref[0].astype(BF16)

    @pl.when(i < nu_ref[0])
    def _():
        x = jnp.concatenate([xs_ref[pl.ds(c, tm, stride=SUBLANES), :] for c in range(SUBLANES)], axis=1)
        u = x * lax.rsqrt(jnp.mean(x * x, axis=-1, keepdims=True) + NORM_EPS) * g_ref[...]
        hgu = _dot(u.astype(BF16), wgu_scr[...]) + bgu_ref[0]
        d = wdn_scr.shape[0]
        gate = jnp.minimum(hgu[:, :d], SWIGLU_LIMIT)
        up = jnp.clip(hgu[:, d:], -SWIGLU_LIMIT, SWIGLU_LIMIT)
        act = (up + 1.0) * (gate * jax.nn.sigmoid(gate * SWIGLU_ALPHA))
        y = _dot(act.astype(BF16), wdn_scr[...]) + bdn_ref[0]
        for c in range(SUBLANES):
            ys_ref[pl.ds(c, tm, stride=SUBLANES), :] = y[:, c * LANES:(c + 1) * LANES]

    @pl.when(i >= nu_ref[0])
    def _():
        ys_ref[...] = jnp.zeros_like(ys_ref)


def _experts(tile_expert, n_used, xs, g, wgu, bgu, wdn, bdn, tm):
    n_tiles = xs.shape[0] // (tm * SUBLANES)
    wsel = lambda shape: pl.BlockSpec(shape, lambda i, te, nu: (te[i], 0, 0))
    used = lambda i, te, nu: (jnp.minimum(i, nu[0] - 1), 0)
    return pl.pallas_call(
        _experts_body,
        grid_spec=pltpu.PrefetchScalarGridSpec(
            num_scalar_prefetch=2,
            grid=(n_tiles,),
            in_specs=[pl.BlockSpec((tm * SUBLANES, LANES), used),
                      pl.BlockSpec((1, D_MODEL), lambda i, te, nu: (0, 0)),
                      wsel((1, D_MODEL, 2 * D_MODEL)), wsel((1, 1, 2 * D_MODEL)),
                      wsel((1, D_MODEL, D_MODEL)), wsel((1, 1, D_MODEL))],
            out_specs=pl.BlockSpec((tm * SUBLANES, LANES), lambda i, te, nu: (i, 0)),
            scratch_shapes=[pltpu.VMEM((D_MODEL, 2 * D_MODEL), BF16), pltpu.VMEM((D_MODEL, D_MODEL), BF16)],
        ),
        out_shape=jax.ShapeDtypeStruct(xs.shape, F32),
        compiler_params=pltpu.CompilerParams(
            dimension_semantics=("arbitrary",), vmem_limit_bytes=VMEM_LIMIT),
        name="experts",
    )(tile_expert, n_used, xs, g, wgu, bgu, wdn, bdn)


def _combine_body(dcur_ref, dnext_ref, prob_ref, h_ref, gf_ref, ys_ref, o_ref, ybuf, sem):
    i = pl.program_id(0)
    n = pl.num_programs(0)
    tm = h_ref.shape[0]
    slot = i % 2

    def gather(dref, s):
        def issue(t, carry):
            row = pl.multiple_of(t * SUBLANES, SUBLANES)
            for k in range(TOP_K):
                d = pl.multiple_of(dref[0, t * TOP_K + k] * SUBLANES, SUBLANES)
                pltpu.make_async_copy(ys_ref.at[pl.ds(d, SUBLANES)], ybuf.at[s, k, pl.ds(row, SUBLANES)],
                                      sem.at[s]).start(priority=k % 2)
            return carry
        lax.fori_loop(0, tm, issue, 0, unroll=DMA_UNROLL)

    @pl.when(i == 0)
    def _():
        gather(dcur_ref, 0)

    @pl.when(i + 1 < n)
    def _():
        gather(dnext_ref, 1 - slot)

    for k in range(TOP_K):
        pltpu.make_async_copy(ys_ref.at[pl.ds(0, tm * SUBLANES)], ybuf.at[slot, k], sem.at[slot]).wait()
    prob = prob_ref[...]
    cols = []
    for c in range(SUBLANES):
        acc = h_ref[:, c * LANES:(c + 1) * LANES]
        for k in range(TOP_K):
            acc = acc + prob[:, k:k + 1] * ybuf[slot, k, pl.ds(c, tm, stride=SUBLANES), :]
        cols.append(acc)
    h = jnp.concatenate(cols, axis=1)
    o_ref[...] = h * lax.rsqrt(jnp.mean(h * h, axis=-1, keepdims=True) + NORM_EPS) * gf_ref[...]


def _combine(dest2, prob, h2, gf, ys, tm):
    t = h2.shape[0]
    n = t // tm
    smem = lambda imap: pl.BlockSpec((None, 1, tm * TOP_K), imap, memory_space=pltpu.SMEM)
    return pl.pallas_call(
        _combine_body,
        grid=(n,),
        in_specs=[smem(lambda i: (i, 0, 0)), smem(lambda i: (jnp.minimum(i + 1, n - 1), 0, 0)),
                  pl.BlockSpec((tm, LANES), lambda i: (i, 0)),
                  pl.BlockSpec((tm, D_MODEL), lambda i: (i, 0)),
                  pl.BlockSpec((1, D_MODEL), lambda i: (0, 0)),
                  pl.BlockSpec(memory_space=pl.ANY)],
        out_specs=pl.BlockSpec((tm, D_MODEL), lambda i: (i, 0)),
        out_shape=jax.ShapeDtypeStruct((t, D_MODEL), F32),
        scratch_shapes=[pltpu.VMEM((2, TOP_K, tm * SUBLANES, LANES), F32), pltpu.SemaphoreType.DMA((2,))],
        compiler_params=pltpu.CompilerParams(
            dimension_semantics=("arbitrary",), vmem_limit_bytes=VMEM_LIMIT),
        name="combine",
    )(dest2, dest2, prob, h2, gf, ys)


def _moe(h2, g, wr3, br, wgu, bgu, wdn, bdn, gf):
    t = h2.shape[0]
    tm_x = _pick_tile(t, (512, 256, 128))
    tm_c = _pick_tile(t, (256, 128))
    idx, prob, rank, cnt = _route(h2, g, wr3, br, tm_x)

    counts = cnt[0, :N_EXPERTS].astype(jnp.int32)
    padded = (counts + tm_x - 1) // tm_x * tm_x
    ends = jnp.cumsum(padded)
    offs = ends - padded
    ns = t * TOP_K + (N_EXPERTS + 1) * tm_x
    e_ids = jnp.arange(N_EXPERTS, dtype=jnp.int32)
    idx4 = idx[:, :TOP_K]
    dest = rank[:, :TOP_K] + jnp.sum(jnp.where(idx4[..., None] == e_ids, offs, 0), axis=-1)
    tile_start = jnp.arange(ns // tm_x, dtype=jnp.int32) * tm_x
    tile_expert = jnp.minimum(jnp.sum((tile_start[:, None] >= ends[None, :]).astype(jnp.int32), axis=1),
                              N_EXPERTS - 1)
    n_used = (ends[-1] // tm_x).reshape(1)

    xs = _dispatch(jnp.concatenate([offs + counts, n_used]), dest.reshape(t // tm_x, 1, tm_x * TOP_K), h2, ns, tm_x)
    ys = _experts(tile_expert, n_used, xs, g, wgu, bgu, wdn, bdn, tm_x)
    return _combine(dest.reshape(t // tm_c, 1, tm_c * TOP_K), prob, h2, gf, ys, tm_c)


def _pick_tile(n, prefs):
    for t in prefs:
        if n % t == 0:
            return t
    return n


def _pad_rows(w, rows, offset=0):
    out = jnp.zeros((rows, w.shape[1]), w.dtype)
    return out.at[offset:offset + w.shape[0]].set(w)


def kernel(x, meta_tokens, norm_mix_g, w_in, w_gla_a2, b_gla_a, gla_norm_g, w_gla_o, mu_r, mu_k, mu_v, mu_w, mu_a, mu_g, w_decay2, b_decay, w_a2, b_a, w_gate2, k_k, k_a, r_k, ln_x_g, ln_x_b, w_rwkv_o, w_out, norm_ffn_g, w_router, b_router, w_exp_gu, b_exp_gu, w_exp_down, b_exp_down, norm_final_g):
    batch, seq, d = x.shape
    assert d == D_MODEL and seq % CHUNK == 0 and w_in.shape[0] == 1
    lp = FRONT_PAD + N_META + seq
    row = lambda a: a.reshape(1, -1)

    meta = jnp.broadcast_to(meta_tokens[None].astype(x.dtype), (batch, N_META, d))
    hp = jnp.concatenate([jnp.zeros((batch, FRONT_PAD, d), x.dtype), meta, x], axis=1).reshape(batch * lp, d)

    splits = (GLA_KEY, GLA_KEY, D_MODEL, D_MODEL, GLA_GATE_RANK, D_MODEL, D_MODEL, D_MODEL,
              RWKV_DECAY_RANK, RWKV_A_RANK, RWKV_GATE_RANK, D_MODEL, D_MODEL)
    offs = [0]
    for s in splits:
        offs.append(offs[-1] + s)
    piece = lambda i: w_in[0][:, offs[i]:offs[i + 1]]
    zcols = lambda n: jnp.zeros((d, n), w_in.dtype)
    small = jnp.concatenate([piece(4), zcols(LANES - GLA_GATE_RANK), piece(8), piece(9), piece(10),
                             zcols(2 * LANES - RWKV_GATE_RANK)], axis=1)
    w_big = jnp.concatenate([piece(0), piece(1), piece(2), piece(3), piece(5), piece(6), piece(7),
                             piece(11), piece(12), small], axis=1).astype(BF16)

    m = batch * lp
    tm = _pick_tile(m, (1280, 640, 512, 320, 256, 192, 128, 64))
    tn = _pick_tile(NP_COLS, (2176, 512))
    p = _inproj(hp, row(norm_mix_g[0]), w_big, tm, tn)

    p3 = p.reshape(batch, lp, NP_COLS)
    nb = _pick_tile(batch, (SEQS_PER_STEP, 1))
    mg = _gla(p3, _pad_rows(w_gla_a2[0], LANES).astype(BF16), row(b_gla_a[0]), row(gla_norm_g[0]),
              w_gla_o[0].astype(BF16), nb)

    pv = lambda a: a.reshape(RWKV_PAIRS, 1, LANES)
    mu_small = jnp.concatenate([jnp.zeros((LANES,), F32), mu_w[0], mu_a[0], mu_g[0],
                                jnp.zeros((2 * LANES - RWKV_GATE_RANK,), F32)])
    prm = (row(mu_r[0]), row(mu_k[0]), row(mu_v[0]), row(mu_small),
           _pad_rows(w_decay2[0], LANES).astype(BF16), row(b_decay[0]),
           _pad_rows(w_a2[0], LANES, RWKV_DECAY_RANK).astype(BF16), row(b_a[0]),
           _pad_rows(w_gate2[0], 2 * LANES).astype(BF16),
           pv(k_k[0]), pv(k_a[0]), pv(r_k[0]), pv(ln_x_g[0]), pv(ln_x_b[0]),
           w_rwkv_o[0].astype(BF16), w_out[0].astype(BF16))
    h2 = _rwkv(p3, mg, x, prm, nb).reshape(batch * seq, d)

    wr = jnp.zeros((d, LANES), F32).at[:, :N_EXPERTS].set(w_router[0])
    wr3 = jnp.stack(_split3(wr))
    br = jnp.zeros((1, LANES), F32).at[0, :N_EXPERTS].set(b_router[0])
    out = _moe(h2, row(norm_ffn_g[0]), wr3, br, w_exp_gu[0], b_exp_gu[0][:, None, :],
               w_exp_down[0], b_exp_down[0][:, None, :], row(norm_final_g))
    return out.reshape(batch, seq, d)
```

```python
import math

import jax
import jax.numpy as jnp
from jax import lax
from jax.experimental import pallas as pl
from jax.experimental.pallas import tpu as pltpu

F32 = jnp.float32
BF16 = jnp.bfloat16

D_MODEL = 1024
N_META = 16
NORM_EPS = 1e-5
CHUNK = 64
FRONT_PAD = (-N_META) % CHUNK
GLA_HEADS = 4
GLA_DK = 128
GLA_DV = 256
GLA_KEY = GLA_HEADS * GLA_DK
GLA_GATE_RANK = 16
GLA_GATE_NORM = 16.0
RWKV_HEAD = 64
RWKV_PAIRS = D_MODEL // (2 * RWKV_HEAD)
RWKV_DECAY_RANK = 64
RWKV_A_RANK = 64
RWKV_GATE_RANK = 160
RWKV_GN_EPS = 64e-5
RWKV_DECAY_SCALE = math.exp(-0.5)
N_EXPERTS = 32
TOP_K = 4
SWIGLU_LIMIT = 7.0
SWIGLU_ALPHA = 1.702
GLA_SEQS_PER_STEP = 4
RWKV_SEQS_PER_STEP = 2
DMA_UNROLL = 8
LANES = 128
SUBLANES = 8
SMALL_W = 512
COL_GQ, COL_GK, COL_GV, COL_RR, COL_RK, COL_RV = 0, 512, 1024, 2048, 3072, 4096
P16_COLS = COL_RV + D_MODEL
COL_GR, COL_GATE_GLA, COL_GATE_RWKV, COL_SMALL = 0, 1024, 2048, 3072
P32_COLS = COL_SMALL + SMALL_W
P16_TILE, P32_TILE = 1280, 1792
VMEM_LIMIT = 56 * 1024 * 1024


def _dot(a, b):
    return jnp.dot(a, b, preferred_element_type=F32)


def _bmm(a, b):
    return jnp.einsum("gik,gkj->gij", a, b, preferred_element_type=F32)


def _bmm_nt(a, b):
    return jnp.einsum("gik,gjk->gij", a, b, preferred_element_type=F32)


def _bmm_tn(a, b):
    return jnp.einsum("gti,gtj->gij", a, b, preferred_element_type=F32)


def _split2(x):
    hi = x.astype(BF16)
    lo = (x - hi.astype(F32)).astype(BF16)
    return hi, lo


def _split3(x):
    hi = x.astype(BF16)
    r1 = x - hi.astype(F32)
    mid = r1.astype(BF16)
    lo = (r1 - mid.astype(F32)).astype(BF16)
    return hi, mid, lo


def _cumsum_rows(x, tri, split):
    parts = split(x)
    out = _dot(tri, parts[0])
    for p in parts[1:]:
        out = out + _dot(tri, p)
    return out


def _softplus(x):
    return jnp.maximum(x, 0.0) + jnp.log1p(jnp.exp(-jnp.abs(x)))


def _tri_blocks(rows, blk):
    r = lax.broadcasted_iota(jnp.int32, (rows, rows), 0)
    c = lax.broadcasted_iota(jnp.int32, (rows, rows), 1)
    return ((r >= c) & (r // blk == c // blk)).astype(BF16)


def _split_lanes(x, n, w):
    s, c, _ = x.shape
    return jnp.stack([x[:, :, j * w:(j + 1) * w] for j in range(n)], axis=1).reshape(s * n, c, w)


def _merge_lanes(x, n):
    sn, c, w = x.shape
    x = x.reshape(sn // n, n, c, w)
    return jnp.concatenate([x[:, j] for j in range(n)], axis=-1).reshape(sn // n * c, n * w)


def _inproj_body(x_ref, g_ref, w16_ref, w32_ref, o16_ref, o32_ref, u_scr):
    j = pl.program_id(1)
    n16 = P16_COLS // P16_TILE

    @pl.when(j == 0)
    def _():
        x = x_ref[...]
        ms = jnp.mean(x * x, axis=-1, keepdims=True)
        u_scr[...] = (x * lax.rsqrt(ms + NORM_EPS) * g_ref[...]).astype(BF16)

    @pl.when(j < n16)
    def _():
        o16_ref[...] = _dot(u_scr[...], w16_ref[...]).astype(BF16)

    @pl.when(j >= n16)
    def _():
        o32_ref[...] = _dot(u_scr[...], w32_ref[...])


def _inproj(hp, g, w16, w32, tm):
    m = hp.shape[0]
    n16 = P16_COLS // P16_TILE
    n32 = P32_COLS // P32_TILE
    c16 = lambda j: jnp.minimum(j, n16 - 1)
    c32 = lambda j: jnp.maximum(j - n16, 0)
    return pl.pallas_call(
        _inproj_body,
        grid=(m // tm, n16 + n32),
        in_specs=[
            pl.BlockSpec((tm, D_MODEL), lambda i, j: (i, 0)),
            pl.BlockSpec((1, D_MODEL), lambda i, j: (0, 0)),
            pl.BlockSpec((D_MODEL, P16_TILE), lambda i, j: (0, c16(j))),
            pl.BlockSpec((D_MODEL, P32_TILE), lambda i, j: (0, c32(j))),
        ],
        out_specs=[pl.BlockSpec((tm, P16_TILE), lambda i, j: (i, c16(j))),
                   pl.BlockSpec((tm, P32_TILE), lambda i, j: (i, c32(j)))],
        out_shape=[jax.ShapeDtypeStruct((m, P16_COLS), BF16), jax.ShapeDtypeStruct((m, P32_COLS), F32)],
        scratch_shapes=[pltpu.VMEM((tm, D_MODEL), BF16)],
        compiler_params=pltpu.CompilerParams(
            dimension_semantics=("parallel", "arbitrary"), vmem_limit_bytes=VMEM_LIMIT),
        name="inproj",
    )(hp, g, w16, w32)


def _gla_body(q_ref, k_ref, v_ref, r_ref, sm_ref, gate_ref, wa2_ref, ba_ref, ng_ref, wo_ref,
              o_ref, s_scr):
    c = pl.program_id(1)
    nb = q_ref.shape[0]
    C = CHUNK
    rows = nb * C
    H = GLA_HEADS

    @pl.when(c == 0)
    def _():
        s_scr[...] = jnp.zeros_like(s_scr)

    row = lax.broadcasted_iota(jnp.int32, (rows, 1), 0) % C
    causal = (lax.broadcasted_iota(jnp.int32, (C, C), 0) >= lax.broadcasted_iota(jnp.int32, (C, C), 1))[None]

    z = _dot(sm_ref[...].reshape(rows, SMALL_W)[:, 0:LANES].astype(BF16), wa2_ref[...]) + ba_ref[...]
    gk = -_softplus(-z) * (1.0 / GLA_GATE_NORM)
    gk = jnp.where((c > 0) | (row >= FRONT_PAD), gk, 0.0)
    b = _cumsum_rows(gk, _tri_blocks(rows, C), _split3).reshape(nb, C, GLA_KEY)
    b_ref = b[:, C // 2 - 1:C // 2, :]
    b_last = b[:, C - 1:C, :]

    q = q_ref[...].astype(F32) * (GLA_DK ** -0.5)
    k = k_ref[...].astype(F32)
    keys = lambda x: _split_lanes(x, H, GLA_DK)
    vals = lambda x: _split_lanes(x, H, GLA_DV)
    qe = keys((q * jnp.exp(b - b_ref)).astype(BF16))
    ke = keys((k * jnp.exp(b_ref - b)).astype(BF16))
    qs = keys((q * jnp.exp(b)).astype(BF16))
    kd = keys((k * jnp.exp(b_last - b)).astype(BF16))
    decay = keys(jnp.exp(b_last))
    v = vals(v_ref[...])
    silu_r = r_ref[...]
    silu_r = vals(silu_r * jax.nn.sigmoid(silu_r))

    a = jnp.where(causal, _bmm_nt(qe, ke), 0.0)
    st = s_scr[...]
    o = _bmm(a.astype(BF16), v) + _bmm_nt(qs, st.astype(BF16))
    s_scr[...] = st * decay + _bmm_tn(v, kd)
    o = o * lax.rsqrt(jnp.mean(o * o, axis=-1, keepdims=True) + NORM_EPS) * ng_ref[...]
    og = _merge_lanes(o * silu_r, H).astype(BF16)
    out = jax.nn.sigmoid(gate_ref[...].reshape(rows, D_MODEL)) * _dot(og, wo_ref[...])
    o_ref[...] = out.reshape(nb, C, D_MODEL)


def _gla(p16, p32, wa2, ba, ng, wo, nb):
    batch, lp, _ = p16.shape
    n_chunks = lp // CHUNK
    pspec = lambda w, col: pl.BlockSpec((nb, CHUNK, w), lambda b, c: (b, c, col // w))
    full = lambda shape: pl.BlockSpec(shape, lambda b, c: (0,) * len(shape))
    return pl.pallas_call(
        _gla_body,
        grid=(batch // nb, n_chunks),
        in_specs=[
            pspec(GLA_KEY, COL_GQ), pspec(GLA_KEY, COL_GK), pspec(D_MODEL, COL_GV),
            pspec(D_MODEL, COL_GR), pspec(SMALL_W, COL_SMALL), pspec(D_MODEL, COL_GATE_GLA),
            full((LANES, GLA_KEY)), full((1, GLA_KEY)), full((1, GLA_DV)), full((D_MODEL, D_MODEL)),
        ],
        out_specs=pl.BlockSpec((nb, CHUNK, D_MODEL), lambda b, c: (b, jnp.maximum(c - 1, 0), 0)),
        out_shape=jax.ShapeDtypeStruct((batch, (n_chunks - 1) * CHUNK, D_MODEL), F32),
        scratch_shapes=[pltpu.VMEM((nb * GLA_HEADS, GLA_DV, GLA_DK), F32)],
        compiler_params=pltpu.CompilerParams(
            dimension_semantics=("parallel", "arbitrary"), vmem_limit_bytes=VMEM_LIMIT),
        name="gla",
    )(p16, p16, p16, p32, p32, p32, wa2, ba, ng, wo)


def _stack_heads(x, lane_lo):
    zero = jnp.zeros((), x.dtype)
    return jnp.concatenate([jnp.where(lane_lo, x, zero), jnp.where(lane_lo, zero, x)], axis=1)


def _rwkv_body(pr_ref, pk_ref, pv_ref, sm_ref, gate_ref, mg_ref, x_ref,
               mur_ref, muk_ref, muv_ref, musm_ref, wd_ref, bd_ref, wa_ref, ba_ref, wg_ref,
               kk_ref, ka_ref, rk_ref, lng_ref, lnb_ref, wo_ref, wout_ref,
               o_ref,
               shr_scr, shk_scr, shv_scr, shs_scr, s_scr):
    c = pl.program_id(1)
    nb = pr_ref.shape[0]
    C = CHUNK
    rows = nb * C
    G = nb * RWKV_PAIRS

    @pl.when(c == 0)
    def _():
        s_scr[...] = jnp.zeros_like(s_scr)
        for scr in (shr_scr, shk_scr, shv_scr, shs_scr):
            scr[:, 7:8, :] = jnp.zeros((nb, 1, scr.shape[2]), F32)

    def lerp(x_ref_, scr, mu_ref):
        x = x_ref_[...].astype(F32)
        scr[:, 8:8 + C, :] = x
        prev = scr[:, 7:7 + C, :]
        scr[:, 7:8, :] = x[:, C - 1:C, :]
        return x + (prev - x) * mu_ref[...]

    pairs = lambda x: _split_lanes(x, RWKV_PAIRS, LANES)
    per_pair = lambda ref: jnp.broadcast_to(ref[...][None], (nb, RWKV_PAIRS, 1, LANES)).reshape(G, 1, LANES)

    r = lerp(pr_ref, shr_scr, mur_ref)
    k = lerp(pk_ref, shk_scr, muk_ref)
    v = lerp(pv_ref, shv_scr, muv_ref)
    ls = lerp(sm_ref, shs_scr, musm_ref).reshape(rows, SMALL_W)
    s1 = ls[:, LANES:2 * LANES]
    xw = bd_ref[...] + _dot(jnp.tanh(s1).astype(BF16), wd_ref[...])
    logw = -RWKV_DECAY_SCALE * jax.nn.sigmoid(xw)
    a = jax.nn.sigmoid(ba_ref[...] + _dot(s1.astype(BF16), wa_ref[...]))
    g = _dot(jax.nn.sigmoid(ls[:, 2 * LANES:4 * LANES]).astype(BF16), wg_ref[...])

    cl = _cumsum_rows(logw, _tri_blocks(rows, C), _split2).reshape(nb, C, D_MODEL)
    logw = logw.reshape(nb, C, D_MODEL)
    cref = cl[:, C // 2 - 1:C // 2, :]
    clast = cl[:, C - 1:C, :]
    e_neg = pairs(jnp.exp(cref - cl))
    e_prev = pairs(jnp.exp(cl - logw - cref))
    e_cur = pairs(jnp.exp(cl - cref))
    g_last = pairs(jnp.exp(clast - cref))
    g_ref = pairs(jnp.exp(cref))
    g_c = pairs(jnp.exp(clast))
    r = pairs(r)
    k = pairs(k)
    v = pairs(v)
    a = pairs(a.reshape(nb, C, D_MODEL))

    lane_lo = lax.broadcasted_iota(jnp.int32, (1, C, LANES), 2) < RWKV_HEAD
    er = lax.broadcasted_iota(jnp.int32, (LANES, LANES), 0)
    ec = lax.broadcasted_iota(jnp.int32, (LANES, LANES), 1)
    seg_ones = ((er // RWKV_HEAD) == (ec // RWKV_HEAD)).astype(BF16)
    strict = ((er % C) > (ec % C))[None]
    incl = ((er % C) >= (ec % C))[None]

    def seg(x):
        return _dot(x.reshape(G * C, LANES).astype(BF16), seg_ones).reshape(G, C, LANES)

    kk = k * per_pair(kk_ref)
    kk = kk * lax.rsqrt(jnp.maximum(seg(kk * kk), 1e-24))
    k2 = k * (1.0 + (a - 1.0) * per_pair(ka_ref))
    bonus = seg(r * k2 * per_pair(rk_ref))
    at = -kk * e_prev
    rt = r * e_cur
    bt = kk * a * e_neg
    kt = k2 * e_neg
    st = s_scr[...]
    sp = (st * g_ref).astype(BF16)

    stk = lambda z: _stack_heads(z.astype(BF16), lane_lo)
    lhs = jnp.concatenate([stk(at), stk(rt)], axis=1)
    rhs = jnp.concatenate([stk(bt), stk(kt)], axis=1)
    m1 = _bmm_nt(lhs, rhs).astype(BF16)
    n2 = 2 * C
    ab = jnp.where(strict, m1[:, :n2, :n2], 0.0)
    ak = jnp.where(strict, m1[:, :n2, n2:], 0.0)
    rb = jnp.where(incl, m1[:, n2:, :n2], 0.0)
    rk = jnp.where(incl, m1[:, n2:, n2:], 0.0)
    vs = stk(v)

    pm = _bmm_nt(lhs[:, :n2], sp) + _bmm(ak, vs)
    x = ab
    for i in range(6):
        pm = pm + _bmm(x, pm.astype(BF16))
        if i < 5:
            x = _bmm(x, x).astype(BF16)
    pmb = pm.astype(BF16)
    y2 = _bmm_nt(lhs[:, n2:], sp) + _bmm(rb, pmb) + _bmm(rk, vs)
    y = y2[:, :C] + y2[:, C:]

    upd_l = jnp.concatenate([pmb, vs], axis=1)
    upd_r = jnp.concatenate([stk(bt * g_last), stk(kt * g_last)], axis=1)
    s_scr[...] = st * g_c + _bmm_tn(upd_l, upd_r)

    mean = seg(y) * (1.0 / RWKV_HEAD)
    yc = y - mean
    var = seg(yc * yc) * (1.0 / RWKV_HEAD)
    y = yc * lax.rsqrt(var + RWKV_GN_EPS) * per_pair(lng_ref) + per_pair(lnb_ref) + bonus * v

    y = _merge_lanes(y, RWKV_PAIRS)
    orw = _dot((y * g).astype(BF16), wo_ref[...])
    mix = mg_ref[...].reshape(rows, D_MODEL) + jax.nn.sigmoid(gate_ref[...].reshape(rows, D_MODEL)) * orw
    out = x_ref[...].reshape(rows, D_MODEL) + _dot(mix.astype(BF16), wout_ref[...])
    o_ref[...] = out.reshape(nb, C, D_MODEL)


def _rwkv(p16, p32, mg, x, prm, nb):
    batch, lp, _ = p16.shape
    n_chunks = lp // CHUNK
    real = lambda b, c: (b, jnp.maximum(c - 1, 0), 0)
    pspec = lambda w, col: pl.BlockSpec((nb, CHUNK, w), lambda b, c: (b, c, col // w))
    full = lambda shape: pl.BlockSpec(shape, lambda b, c: (0,) * len(shape))
    pairvec = full((RWKV_PAIRS, 1, LANES))
    vec = full((1, D_MODEL))
    shift = lambda w: pltpu.VMEM((nb, CHUNK + SUBLANES, w), F32)
    return pl.pallas_call(
        _rwkv_body,
        grid=(batch // nb, n_chunks),
        in_specs=[
            pspec(D_MODEL, COL_RR), pspec(D_MODEL, COL_RK), pspec(D_MODEL, COL_RV),
            pspec(SMALL_W, COL_SMALL), pspec(D_MODEL, COL_GATE_RWKV),
            pl.BlockSpec((nb, CHUNK, D_MODEL), real), pl.BlockSpec((nb, CHUNK, D_MODEL), real),
            vec, vec, vec, full((1, SMALL_W)),
            full((LANES, D_MODEL)), vec, full((LANES, D_MODEL)), vec, full((2 * LANES, D_MODEL)),
            pairvec, pairvec, pairvec, pairvec, pairvec,
            full((D_MODEL, D_MODEL)), full((D_MODEL, D_MODEL)),
        ],
        out_specs=pl.BlockSpec((nb, CHUNK, D_MODEL), real),
        out_shape=jax.ShapeDtypeStruct(x.shape, F32),
        scratch_shapes=[shift(D_MODEL), shift(D_MODEL), shift(D_MODEL), shift(SMALL_W),
                        pltpu.VMEM((nb * RWKV_PAIRS, LANES, LANES), F32)],
        compiler_params=pltpu.CompilerParams(
            dimension_semantics=("parallel", "arbitrary"), vmem_limit_bytes=VMEM_LIMIT),
        name="rwkv",
    )(p16, p16, p16, p32, p32, mg, x, *prm)


def _route_body(h_ref, g_ref, wr_ref, br_ref, idx_ref, prob_ref, rank_ref, cnt_ref, base_scr):
    tm = h_ref.shape[0]

    @pl.when(pl.program_id(0) == 0)
    def _():
        base_scr[...] = jnp.zeros_like(base_scr)

    h = h_ref[...]
    u = h * lax.rsqrt(jnp.mean(h * h, axis=-1, keepdims=True) + NORM_EPS) * g_ref[...]
    hi, mid, lo = _split3(u)
    whi, wmid, wlo = wr_ref[0], wr_ref[1], wr_ref[2]
    logits = (_dot(hi, whi) + (_dot(hi, wmid) + _dot(mid, whi))
              + (_dot(hi, wlo) + _dot(mid, wmid) + _dot(lo, whi))) + br_ref[...]
    lane = lax.broadcasted_iota(jnp.int32, (tm, LANES), 1)
    logits = jnp.where(lane < N_EXPERTS, logits, -jnp.inf)
    idx_out = jnp.zeros((tm, LANES), jnp.int32)
    prob_out = jnp.zeros((tm, LANES), F32)
    denom = jnp.zeros((tm, 1), F32)
    hits = []
    top = None
    for k in range(TOP_K):
        m = jnp.max(logits, axis=-1, keepdims=True)
        idx = jnp.min(jnp.where(logits == m, lane, LANES), axis=-1, keepdims=True)
        hit = lane == idx
        top = m if top is None else top
        w = jnp.exp(m - top)
        idx_out = jnp.where(lane == k, idx, idx_out)
        prob_out = jnp.where(lane == k, w, prob_out)
        denom = denom + w
        hits.append(hit)
        logits = jnp.where(hit, -jnp.inf, logits)
    idx_ref[...] = idx_out
    prob_ref[...] = prob_out / denom

    tot = jnp.zeros((tm, LANES), F32)
    for hit in hits:
        tot = tot + jnp.where(hit, 1.0, 0.0)
    r = lax.broadcasted_iota(jnp.int32, (tm, tm), 0)
    c = lax.broadcasted_iota(jnp.int32, (tm, tm), 1)
    before = _dot((r > c).astype(BF16), tot.astype(BF16)) + base_scr[...]
    rank_out = jnp.zeros((tm, LANES), F32)
    for k, hit in enumerate(hits):
        rk = jnp.sum(jnp.where(hit, before, 0.0), axis=-1, keepdims=True)
        rank_out = jnp.where(lane == k, rk, rank_out)
    rank_ref[...] = rank_out.astype(jnp.int32)
    base_scr[...] += jnp.sum(tot, axis=0, keepdims=True)
    cnt_ref[...] = base_scr[...]


def _route(h2, g, wr3, br, tm):
    t = h2.shape[0]
    full = lambda shape: pl.BlockSpec(shape, lambda i: (0,) * len(shape))
    tile = lambda w: pl.BlockSpec((tm, w), lambda i: (i, 0))
    return pl.pallas_call(
        _route_body,
        grid=(t // tm,),
        in_specs=[tile(D_MODEL), full((1, D_MODEL)), full((3, D_MODEL, LANES)), full((1, LANES))],
        out_specs=[tile(LANES), tile(LANES), tile(LANES), full((1, LANES))],
        out_shape=[jax.ShapeDtypeStruct((t, LANES), jnp.int32),
                   jax.ShapeDtypeStruct((t, LANES), F32),
                   jax.ShapeDtypeStruct((t, LANES), jnp.int32),
                   jax.ShapeDtypeStruct((1, LANES), F32)],
        scratch_shapes=[pltpu.VMEM((1, LANES), F32)],
        compiler_params=pltpu.CompilerParams(
            dimension_semantics=("arbitrary",), vmem_limit_bytes=VMEM_LIMIT),
        name="route",
    )(h2, g, wr3, br)


def _dispatch_body(tail_ref, dest_ref, h_ref, xs_ref, hs_scr, sem):
    tm = h_ref.shape[0]
    zrows = hs_scr.shape[0]

    @pl.when(pl.program_id(0) == 0)
    def _():
        hs_scr[...] = jnp.zeros_like(hs_scr)
        for e in range(N_EXPERTS):
            start = pl.multiple_of(tail_ref[e] * SUBLANES, SUBLANES)
            pltpu.make_async_copy(hs_scr, xs_ref.at[pl.ds(start, zrows)], sem).start()
        for e in range(N_EXPERTS):
            pltpu.make_async_copy(hs_scr, xs_ref.at[pl.ds(0, zrows)], sem).wait()

        def zero_tile(j, carry):
            start = pl.multiple_of(j * zrows, zrows)
            pltpu.make_async_copy(hs_scr, xs_ref.at[pl.ds(start, zrows)], sem).start()
            return carry

        def wait_tile(j, carry):
            pltpu.make_async_copy(hs_scr, xs_ref.at[pl.ds(0, zrows)], sem).wait()
            return carry

        n_tiles = xs_ref.shape[0] // zrows
        lax.fori_loop(tail_ref[N_EXPERTS], n_tiles, zero_tile, 0)
        lax.fori_loop(tail_ref[N_EXPERTS], n_tiles, wait_tile, 0)

    for c in range(SUBLANES):
        hs_scr[pl.ds(c, tm, stride=SUBLANES), :] = h_ref[:, c * LANES:(c + 1) * LANES]

    def issue(t, carry):
        src = hs_scr.at[pl.ds(pl.multiple_of(t * SUBLANES, SUBLANES), SUBLANES)]
        for k in range(TOP_K):
            d = pl.multiple_of(dest_ref[0, t * TOP_K + k] * SUBLANES, SUBLANES)
            pltpu.make_async_copy(src, xs_ref.at[pl.ds(d, SUBLANES)], sem).start(priority=k % 2)
        return carry

    lax.fori_loop(0, tm, issue, 0, unroll=DMA_UNROLL)
    for k in range(TOP_K):
        pltpu.make_async_copy(hs_scr, xs_ref.at[pl.ds(0, tm * SUBLANES)], sem).wait()


def _dispatch(tail, dest2, h2, ns, tm):
    t = h2.shape[0]
    return pl.pallas_call(
        _dispatch_body,
        grid_spec=pltpu.PrefetchScalarGridSpec(
            num_scalar_prefetch=1,
            grid=(t // tm,),
            in_specs=[pl.BlockSpec((None, 1, tm * TOP_K), lambda i, tl: (i, 0, 0), memory_space=pltpu.SMEM),
                      pl.BlockSpec((tm, D_MODEL), lambda i, tl: (i, 0))],
            out_specs=pl.BlockSpec(memory_space=pl.ANY),
            scratch_shapes=[pltpu.VMEM((tm * SUBLANES, LANES), F32), pltpu.SemaphoreType.DMA(())],
        ),
        out_shape=jax.ShapeDtypeStruct((ns * SUBLANES, LANES), F32),
        compiler_params=pltpu.CompilerParams(
            dimension_semantics=("arbitrary",), vmem_limit_bytes=VMEM_LIMIT),
        name="dispatch",
    )(tail, dest2, h2)


def _experts_body(te_ref, nu_ref, xs_ref, g_ref, wgu_ref, bgu_ref, wdn_ref, bdn_ref, ys_ref,
                  wgu_scr, wdn_scr):
    i = pl.program_id(0)
    tm = xs_ref.shape[0] // SUBLANES

    @pl.when((i == 0) | (te_ref[i] != te_ref[jnp.maximum(i - 1, 0)]))
    def _():
        wgu_scr[...] = wgu_ref[0].astype(BF16)
        wdn_scr[...] = wdn_ref[0].astype(BF16)

    @pl.when(i < nu_ref[0])
    def _():
        x = jnp.concatenate([xs_ref[pl.ds(c, tm, stride=SUBLANES), :] for c in range(SUBLANES)], axis=1)
        u = x * lax.rsqrt(jnp.mean(x * x, axis=-1, keepdims=True) + NORM_EPS) * g_ref[...]
        hgu = _dot(u.astype(BF16), wgu_scr[...]) + bgu_ref[0]
        d = wdn_scr.shape[0]
        gate = jnp.minimum(hgu[:, :d], SWIGLU_LIMIT)
        up = jnp.clip(hgu[:, d:], -SWIGLU_LIMIT, SWIGLU_LIMIT)
        act = (up + 1.0) * (gate * jax.nn.sigmoid(gate * SWIGLU_ALPHA))
        y = _dot(act.astype(BF16), wdn_scr[...]) + bdn_ref[0]
        for c in range(SUBLANES):
            ys_ref[pl.ds(c, tm, stride=SUBLANES), :] = y[:, c * LANES:(c + 1) * LANES]

    @pl.when(i >= nu_ref[0])
    def _():
        ys_ref[...] = jnp.zeros_like(ys_ref)


def _experts(tile_expert, n_used, xs, g, wgu, bgu, wdn, bdn, tm):
    n_tiles = xs.shape[0] // (tm * SUBLANES)
    wsel = lambda shape: pl.BlockSpec(shape, lambda i, te, nu: (te[i], 0, 0))
    used = lambda i, te, nu: (jnp.minimum(i, nu[0] - 1), 0)
    return pl.pallas_call(
        _experts_body,
        grid_spec=pltpu.PrefetchScalarGridSpec(
            num_scalar_prefetch=2,
            grid=(n_tiles,),
            in_specs=[pl.BlockSpec((tm * SUBLANES, LANES), used),
                      pl.BlockSpec((1, D_MODEL), lambda i, te, nu: (0, 0)),
                      wsel((1, D_MODEL, 2 * D_MODEL)), wsel((1, 1, 2 * D_MODEL)),
                      wsel((1, D_MODEL, D_MODEL)), wsel((1, 1, D_MODEL))],
            out_specs=pl.BlockSpec((tm * SUBLANES, LANES), lambda i, te, nu: (i, 0)),
            scratch_shapes=[pltpu.VMEM((D_MODEL, 2 * D_MODEL), BF16), pltpu.VMEM((D_MODEL, D_MODEL), BF16)],
        ),
        out_shape=jax.ShapeDtypeStruct(xs.shape, F32),
        compiler_params=pltpu.CompilerParams(
            dimension_semantics=("arbitrary",), vmem_limit_bytes=VMEM_LIMIT),
        name="experts",
    )(tile_expert, n_used, xs, g, wgu, bgu, wdn, bdn)


def _combine_body(dcur_ref, dnext_ref, prob_ref, h_ref, gf_ref, ys_ref, o_ref, ybuf, sem):
    i = pl.program_id(0)
    n = pl.num_programs(0)
    tm = h_ref.shape[0]
    slot = i % 2

    def gather(dref, s):
        def issue(t, carry):
            row = pl.multiple_of(t * SUBLANES, SUBLANES)
            for k in range(TOP_K):
                d = pl.multiple_of(dref[0, t * TOP_K + k] * SUBLANES, SUBLANES)
                pltpu.make_async_copy(ys_ref.at[pl.ds(d, SUBLANES)], ybuf.at[s, k, pl.ds(row, SUBLANES)],
                                      sem.at[s]).start(priority=k % 2)
            return carry
        lax.fori_loop(0, tm, issue, 0, unroll=DMA_UNROLL)

    @pl.when(i == 0)
    def _():
        gather(dcur_ref, 0)

    @pl.when(i + 1 < n)
    def _():
        gather(dnext_ref, 1 - slot)

    for k in range(TOP_K):
        pltpu.make_async_copy(ys_ref.at[pl.ds(0, tm * SUBLANES)], ybuf.at[slot, k], sem.at[slot]).wait()
    prob = prob_ref[...]
    cols = []
    for c in range(SUBLANES):
        acc = h_ref[:, c * LANES:(c + 1) * LANES]
        for k in range(TOP_K):
            acc = acc + prob[:, k:k + 1] * ybuf[slot, k, pl.ds(c, tm, stride=SUBLANES), :]
        cols.append(acc)
    h = jnp.concatenate(cols, axis=1)
    o_ref[...] = h * lax.rsqrt(jnp.mean(h * h, axis=-1, keepdims=True) + NORM_EPS) * gf_ref[...]


def _combine(dest2, prob, h2, gf, ys, tm):
    t = h2.shape[0]
    n = t // tm
    smem = lambda imap: pl.BlockSpec((None, 1, tm * TOP_K), imap, memory_space=pltpu.SMEM)
    return pl.pallas_call(
        _combine_body,
        grid=(n,),
        in_specs=[smem(lambda i: (i, 0, 0)), smem(lambda i: (jnp.minimum(i + 1, n - 1), 0, 0)),
                  pl.BlockSpec((tm, LANES), lambda i: (i, 0)),
                  pl.BlockSpec((tm, D_MODEL), lambda i: (i, 0)),
                  pl.BlockSpec((1, D_MODEL), lambda i: (0, 0)),
                  pl.BlockSpec(memory_space=pl.ANY)],
        out_specs=pl.BlockSpec((tm, D_MODEL), lambda i: (i, 0)),
        out_shape=jax.ShapeDtypeStruct((t, D_MODEL), F32),
        scratch_shapes=[pltpu.VMEM((2, TOP_K, tm * SUBLANES, LANES), F32), pltpu.SemaphoreType.DMA((2,))],
        compiler_params=pltpu.CompilerParams(
            dimension_semantics=("arbitrary",), vmem_limit_bytes=VMEM_LIMIT),
        name="combine",
    )(dest2, dest2, prob, h2, gf, ys)


def _moe(h2, g, wr3, br, wgu, bgu, wdn, bdn, gf):
    t = h2.shape[0]
    tm_x = _pick_tile(t, (512, 256, 128))
    tm_c = _pick_tile(t, (256, 128))
    idx, prob, rank, cnt = _route(h2, g, wr3, br, tm_x)

    counts = cnt[0, :N_EXPERTS].astype(jnp.int32)
    padded = (counts + tm_x - 1) // tm_x * tm_x
    ends = jnp.cumsum(padded)
    offs = ends - padded
    ns = t * TOP_K + (N_EXPERTS + 1) * tm_x
    e_ids = jnp.arange(N_EXPERTS, dtype=jnp.int32)
    idx4 = idx[:, :TOP_K]
    dest = rank[:, :TOP_K] + jnp.sum(jnp.where(idx4[..., None] == e_ids, offs, 0), axis=-1)
    tile_start = jnp.arange(ns // tm_x, dtype=jnp.int32) * tm_x
    tile_expert = jnp.minimum(jnp.sum((tile_start[:, None] >= ends[None, :]).astype(jnp.int32), axis=1),
                              N_EXPERTS - 1)
    n_used = (ends[-1] // tm_x).reshape(1)

    xs = _dispatch(jnp.concatenate([offs + counts, n_used]), dest.reshape(t // tm_x, 1, tm_x * TOP_K), h2, ns, tm_x)
    ys = _experts(tile_expert, n_used, xs, g, wgu, bgu, wdn, bdn, tm_x)
    return _combine(dest.reshape(t // tm_c, 1, tm_c * TOP_K), prob, h2, gf, ys, tm_c)


def _pick_tile(n, prefs):
    for t in prefs:
        if n % t == 0:
            return t
    return n


def _pad_rows(w, rows, offset=0):
    out = jnp.zeros((rows, w.shape[1]), w.dtype)
    return out.at[offset:offset + w.shape[0]].set(w)


def kernel(x, meta_tokens, norm_mix_g, w_in, w_gla_a2, b_gla_a, gla_norm_g, w_gla_o, mu_r, mu_k, mu_v, mu_w, mu_a, mu_g, w_decay2, b_decay, w_a2, b_a, w_gate2, k_k, k_a, r_k, ln_x_g, ln_x_b, w_rwkv_o, w_out, norm_ffn_g, w_router, b_router, w_exp_gu, b_exp_gu, w_exp_down, b_exp_down, norm_final_g):
    batch, seq, d = x.shape
    assert d == D_MODEL and seq % CHUNK == 0 and w_in.shape[0] == 1
    lp = FRONT_PAD + N_META + seq
    row = lambda a: a.reshape(1, -1)

    meta = jnp.broadcast_to(meta_tokens[None].astype(x.dtype), (batch, N_META, d))
    hp = jnp.concatenate([jnp.zeros((batch, FRONT_PAD, d), x.dtype), meta, x], axis=1).reshape(batch * lp, d)

    splits = (GLA_KEY, GLA_KEY, D_MODEL, D_MODEL, GLA_GATE_RANK, D_MODEL, D_MODEL, D_MODEL,
              RWKV_DECAY_RANK, RWKV_A_RANK, RWKV_GATE_RANK, D_MODEL, D_MODEL)
    offs = [0]
    for s in splits:
        offs.append(offs[-1] + s)
    piece = lambda i: w_in[0][:, offs[i]:offs[i + 1]]
    zcols = lambda n: jnp.zeros((d, n), w_in.dtype)
    small = jnp.concatenate([piece(4), zcols(LANES - GLA_GATE_RANK), piece(8), piece(9), piece(10),
                             zcols(2 * LANES - RWKV_GATE_RANK)], axis=1)
    w16 = jnp.concatenate([piece(0), piece(1), piece(2), piece(5), piece(6), piece(7)], axis=1).astype(BF16)
    w32 = jnp.concatenate([piece(3), piece(11), piece(12), small], axis=1).astype(BF16)

    m = batch * lp
    tm = _pick_tile(m, (1280, 640, 512, 320, 256, 192, 128, 64))
    p16, p32 = _inproj(hp, row(norm_mix_g[0]), w16, w32, tm)
    p16 = p16.reshape(batch, lp, P16_COLS)
    p32 = p32.reshape(batch, lp, P32_COLS)

    mg = _gla(p16, p32, _pad_rows(w_gla_a2[0], LANES).astype(BF16), row(b_gla_a[0]), row(gla_norm_g[0]),
              w_gla_o[0].astype(BF16), _pick_tile(batch, (GLA_SEQS_PER_STEP, 2, 1)))

    pv = lambda a: a.reshape(RWKV_PAIRS, 1, LANES)
    mu_small = jnp.concatenate([jnp.zeros((LANES,), F32), mu_w[0], mu_a[0], mu_g[0],
                                jnp.zeros((2 * LANES - RWKV_GATE_RANK,), F32)])
    prm = (row(mu_r[0]), row(mu_k[0]), row(mu_v[0]), row(mu_small),
           _pad_rows(w_decay2[0], LANES).astype(BF16), row(b_decay[0]),
           _pad_rows(w_a2[0], LANES, RWKV_DECAY_RANK).astype(BF16), row(b_a[0]),
           _pad_rows(w_gate2[0], 2 * LANES).astype(BF16),
           pv(k_k[0]), pv(k_a[0]), pv(r_k[0]), pv(ln_x_g[0]), pv(ln_x_b[0]),
           w_rwkv_o[0].astype(BF16), w_out[0].astype(BF16))
    h2 = _rwkv(p16, p32, mg, x, prm, _pick_tile(batch, (RWKV_SEQS_PER_STEP, 1))).reshape(batch * seq, d)

    wr = jnp.zeros((d, LANES), F32).at[:, :N_EXPERTS].set(w_router[0])
    wr3 = jnp.stack(_split3(wr))
    br = jnp.zeros((1, LANES), F32).at[0, :N_EXPERTS].set(b_router[0])
    out = _moe(h2, row(norm_ffn_g[0]), wr3, br, w_exp_gu[0], b_exp_gu[0][:, None, :],
               w_exp_down[0], b_exp_down[0][:, None, :], row(norm_final_g))
    return out.reshape(batch, seq, d)
```

```python
import math

import jax
import jax.numpy as jnp
from jax import lax
from jax.experimental import pallas as pl
from jax.experimental.pallas import tpu as pltpu

F32 = jnp.float32
BF16 = jnp.bfloat16

D_MODEL = 1024
N_META = 16
NORM_EPS = 1e-5
CHUNK = 64
FRONT_PAD = (-N_META) % CHUNK
GLA_HEADS = 4
GLA_DK = 128
GLA_DV = 256
GLA_KEY = GLA_HEADS * GLA_DK
GLA_GATE_RANK = 16
GLA_GATE_NORM = 16.0
RWKV_HEAD = 64
RWKV_PAIRS = D_MODEL // (2 * RWKV_HEAD)
RWKV_DECAY_RANK = 64
RWKV_A_RANK = 64
RWKV_GATE_RANK = 160
RWKV_GN_EPS = 64e-5
RWKV_DECAY_SCALE = math.exp(-0.5)
N_EXPERTS = 32
TOP_K = 4
SWIGLU_LIMIT = 7.0
SWIGLU_ALPHA = 1.702
GLA_SEQS_PER_STEP = 4
RWKV_SEQS_PER_STEP = 2
DMA_UNROLL = 8
LANES = 128
SUBLANES = 8
SMALL_W = 512
COL_GQ, COL_GK, COL_GV, COL_GR = 0, 512, 1024, 2048
COL_RR, COL_RK, COL_RV = 3072, 4096, 5120
COL_GATE_GLA, COL_GATE_RWKV, COL_SMALL = 6144, 7168, 8192
NP_COLS = COL_SMALL + SMALL_W
VMEM_LIMIT = 56 * 1024 * 1024


def _dot(a, b):
    return jnp.dot(a, b, preferred_element_type=F32)


def _bmm(a, b):
    return jnp.einsum("gik,gkj->gij", a, b, preferred_element_type=F32)


def _bmm_nt(a, b):
    return jnp.einsum("gik,gjk->gij", a, b, preferred_element_type=F32)


def _bmm_tn(a, b):
    return jnp.einsum("gti,gtj->gij", a, b, preferred_element_type=F32)


def _split2(x):
    hi = x.astype(BF16)
    lo = (x - hi.astype(F32)).astype(BF16)
    return hi, lo


def _split3(x):
    hi = x.astype(BF16)
    r1 = x - hi.astype(F32)
    mid = r1.astype(BF16)
    lo = (r1 - mid.astype(F32)).astype(BF16)
    return hi, mid, lo


def _cumsum_rows(x, tri, split):
    parts = split(x)
    out = _dot(tri, parts[0])
    for p in parts[1:]:
        out = out + _dot(tri, p)
    return out


def _softplus(x):
    return jnp.maximum(x, 0.0) + jnp.log1p(jnp.exp(-jnp.abs(x)))


def _tri_blocks(rows, blk):
    r = lax.broadcasted_iota(jnp.int32, (rows, rows), 0)
    c = lax.broadcasted_iota(jnp.int32, (rows, rows), 1)
    return ((r >= c) & (r // blk == c // blk)).astype(BF16)


def _split_lanes(x, n, w):
    s, c, _ = x.shape
    return jnp.stack([x[:, :, j * w:(j + 1) * w] for j in range(n)], axis=1).reshape(s * n, c, w)


def _merge_lanes(x, n):
    sn, c, w = x.shape
    x = x.reshape(sn // n, n, c, w)
    return jnp.concatenate([x[:, j] for j in range(n)], axis=-1).reshape(sn // n * c, n * w)


def _inproj_body(x_ref, g_ref, w_ref, o_ref, u_scr):
    @pl.when(pl.program_id(1) == 0)
    def _():
        x = x_ref[...]
        ms = jnp.mean(x * x, axis=-1, keepdims=True)
        u_scr[...] = (x * lax.rsqrt(ms + NORM_EPS) * g_ref[...]).astype(BF16)

    o_ref[...] = _dot(u_scr[...], w_ref[...])


def _inproj(hp, g, w, tm, tn):
    m = hp.shape[0]
    return pl.pallas_call(
        _inproj_body,
        grid=(m // tm, NP_COLS // tn),
        in_specs=[
            pl.BlockSpec((tm, D_MODEL), lambda i, j: (i, 0)),
            pl.BlockSpec((1, D_MODEL), lambda i, j: (0, 0)),
            pl.BlockSpec((D_MODEL, tn), lambda i, j: (0, j)),
        ],
        out_specs=pl.BlockSpec((tm, tn), lambda i, j: (i, j)),
        out_shape=jax.ShapeDtypeStruct((m, NP_COLS), F32),
        scratch_shapes=[pltpu.VMEM((tm, D_MODEL), BF16)],
        compiler_params=pltpu.CompilerParams(
            dimension_semantics=("parallel", "arbitrary"), vmem_limit_bytes=VMEM_LIMIT),
        name="inproj",
    )(hp, g, w)


def _gla_body(q_ref, k_ref, v_ref, r_ref, sm_ref, gate_ref, wa2_ref, ba_ref, ng_ref, wo_ref,
              o_ref, s_scr):
    c = pl.program_id(1)
    nb = q_ref.shape[0]
    C = CHUNK
    rows = nb * C
    H = GLA_HEADS

    @pl.when(c == 0)
    def _():
        s_scr[...] = jnp.zeros_like(s_scr)

    row = lax.broadcasted_iota(jnp.int32, (rows, 1), 0) % C
    causal = (lax.broadcasted_iota(jnp.int32, (C, C), 0) >= lax.broadcasted_iota(jnp.int32, (C, C), 1))[None]

    z = _dot(sm_ref[...].reshape(rows, SMALL_W)[:, 0:LANES].astype(BF16), wa2_ref[...]) + ba_ref[...]
    gk = -_softplus(-z) * (1.0 / GLA_GATE_NORM)
    gk = jnp.where((c > 0) | (row >= FRONT_PAD), gk, 0.0)
    b = _cumsum_rows(gk, _tri_blocks(rows, C), _split3).reshape(nb, C, GLA_KEY)
    b_ref = b[:, C // 2 - 1:C // 2, :]
    b_last = b[:, C - 1:C, :]

    q = q_ref[...] * (GLA_DK ** -0.5)
    k = k_ref[...]
    keys = lambda x: _split_lanes(x, H, GLA_DK)
    vals = lambda x: _split_lanes(x, H, GLA_DV)
    qe = keys((q * jnp.exp(b - b_ref)).astype(BF16))
    ke = keys((k * jnp.exp(b_ref - b)).astype(BF16))
    qs = keys((q * jnp.exp(b)).astype(BF16))
    kd = keys((k * jnp.exp(b_last - b)).astype(BF16))
    decay = keys(jnp.exp(b_last))
    v = vals(v_ref[...].astype(BF16))
    silu_r = r_ref[...]
    silu_r = vals(silu_r * jax.nn.sigmoid(silu_r))

    a = jnp.where(causal, _bmm_nt(qe, ke), 0.0)
    st = s_scr[...]
    o = _bmm(a.astype(BF16), v) + _bmm_nt(qs, st.astype(BF16))
    s_scr[...] = st * decay + _bmm_tn(v, kd)
    o = o * lax.rsqrt(jnp.mean(o * o, axis=-1, keepdims=True) + NORM_EPS) * ng_ref[...]
    og = _merge_lanes(o * silu_r, H).astype(BF16)
    out = jax.nn.sigmoid(gate_ref[...].reshape(rows, D_MODEL)) * _dot(og, wo_ref[...])
    o_ref[...] = out.reshape(nb, C, D_MODEL)


def _gla(p3, wa2, ba, ng, wo, nb):
    batch, lp, _ = p3.shape
    n_chunks = lp // CHUNK
    pspec = lambda w, col: pl.BlockSpec((nb, CHUNK, w), lambda b, c: (b, c, col // w))
    full = lambda shape: pl.BlockSpec(shape, lambda b, c: (0,) * len(shape))
    return pl.pallas_call(
        _gla_body,
        grid=(batch // nb, n_chunks),
        in_specs=[
            pspec(GLA_KEY, COL_GQ), pspec(GLA_KEY, COL_GK), pspec(D_MODEL, COL_GV),
            pspec(D_MODEL, COL_GR), pspec(SMALL_W, COL_SMALL), pspec(D_MODEL, COL_GATE_GLA),
            full((LANES, GLA_KEY)), full((1, GLA_KEY)), full((1, GLA_DV)), full((D_MODEL, D_MODEL)),
        ],
        out_specs=pl.BlockSpec((nb, CHUNK, D_MODEL), lambda b, c: (b, jnp.maximum(c - 1, 0), 0)),
        out_shape=jax.ShapeDtypeStruct((batch, (n_chunks - 1) * CHUNK, D_MODEL), F32),
        scratch_shapes=[pltpu.VMEM((nb * GLA_HEADS, GLA_DV, GLA_DK), F32)],
        compiler_params=pltpu.CompilerParams(
            dimension_semantics=("parallel", "arbitrary"), vmem_limit_bytes=VMEM_LIMIT),
        name="gla",
    )(p3, p3, p3, p3, p3, p3, wa2, ba, ng, wo)


def _stack_heads(x, lane_lo):
    zero = jnp.zeros((), x.dtype)
    return jnp.concatenate([jnp.where(lane_lo, x, zero), jnp.where(lane_lo, zero, x)], axis=1)


def _rwkv_body(pr_ref, pk_ref, pv_ref, sm_ref, gate_ref, mg_ref, x_ref,
               mur_ref, muk_ref, muv_ref, musm_ref, wd_ref, bd_ref, wa_ref, ba_ref, wg_ref,
               kk_ref, ka_ref, rk_ref, lng_ref, lnb_ref, wo_ref, wout_ref,
               o_ref,
               shr_scr, shk_scr, shv_scr, shs_scr, s_scr):
    c = pl.program_id(1)
    nb = pr_ref.shape[0]
    C = CHUNK
    rows = nb * C
    G = nb * RWKV_PAIRS

    @pl.when(c == 0)
    def _():
        s_scr[...] = jnp.zeros_like(s_scr)
        for scr in (shr_scr, shk_scr, shv_scr, shs_scr):
            scr[:, 7:8, :] = jnp.zeros((nb, 1, scr.shape[2]), F32)

    def lerp(x_ref_, scr, mu_ref):
        x = x_ref_[...]
        scr[:, 8:8 + C, :] = x
        prev = scr[:, 7:7 + C, :]
        scr[:, 7:8, :] = x[:, C - 1:C, :]
        return x + (prev - x) * mu_ref[...]

    pairs = lambda x: _split_lanes(x, RWKV_PAIRS, LANES)
    per_pair = lambda ref: jnp.broadcast_to(ref[...][None], (nb, RWKV_PAIRS, 1, LANES)).reshape(G, 1, LANES)

    r = lerp(pr_ref, shr_scr, mur_ref)
    k = lerp(pk_ref, shk_scr, muk_ref)
    v = lerp(pv_ref, shv_scr, muv_ref)
    ls = lerp(sm_ref, shs_scr, musm_ref).reshape(rows, SMALL_W)
    s1 = ls[:, LANES:2 * LANES]
    xw = bd_ref[...] + _dot(jnp.tanh(s1).astype(BF16), wd_ref[...])
    logw = -RWKV_DECAY_SCALE * jax.nn.sigmoid(xw)
    a = jax.nn.sigmoid(ba_ref[...] + _dot(s1.astype(BF16), wa_ref[...]))
    g = _dot(jax.nn.sigmoid(ls[:, 2 * LANES:4 * LANES]).astype(BF16), wg_ref[...])

    cl = _cumsum_rows(logw, _tri_blocks(rows, C), _split2).reshape(nb, C, D_MODEL)
    logw = logw.reshape(nb, C, D_MODEL)
    cref = cl[:, C // 2 - 1:C // 2, :]
    clast = cl[:, C - 1:C, :]
    e_neg = pairs(jnp.exp(cref - cl))
    e_prev = pairs(jnp.exp(cl - logw - cref))
    e_cur = pairs(jnp.exp(cl - cref))
    g_last = pairs(jnp.exp(clast - cref))
    g_ref = pairs(jnp.exp(cref))
    g_c = pairs(jnp.exp(clast))
    r = pairs(r)
    k = pairs(k)
    v = pairs(v)
    a = pairs(a.reshape(nb, C, D_MODEL))

    lane_lo = lax.broadcasted_iota(jnp.int32, (1, C, LANES), 2) < RWKV_HEAD
    er = lax.broadcasted_iota(jnp.int32, (LANES, LANES), 0)
    ec = lax.broadcasted_iota(jnp.int32, (LANES, LANES), 1)
    seg_ones = ((er // RWKV_HEAD) == (ec // RWKV_HEAD)).astype(BF16)
    strict = ((er % C) > (ec % C))[None]
    incl = ((er % C) >= (ec % C))[None]

    def seg(x):
        return _dot(x.reshape(G * C, LANES).astype(BF16), seg_ones).reshape(G, C, LANES)

    kk = k * per_pair(kk_ref)
    kk = kk * lax.rsqrt(jnp.maximum(seg(kk * kk), 1e-24))
    k2 = k * (1.0 + (a - 1.0) * per_pair(ka_ref))
    bonus = seg(r * k2 * per_pair(rk_ref))
    at = -kk * e_prev
    rt = r * e_cur
    bt = kk * a * e_neg
    kt = k2 * e_neg
    st = s_scr[...]
    sp = (st * g_ref).astype(BF16)

    stk = lambda z: _stack_heads(z.astype(BF16), lane_lo)
    lhs = jnp.concatenate([stk(at), stk(rt)], axis=1)
    rhs = jnp.concatenate([stk(bt), stk(kt)], axis=1)
    m1 = _bmm_nt(lhs, rhs).astype(BF16)
    n2 = 2 * C
    ab = jnp.where(strict, m1[:, :n2, :n2], 0.0)
    ak = jnp.where(strict, m1[:, :n2, n2:], 0.0)
    rb = jnp.where(incl, m1[:, n2:, :n2], 0.0)
    rk = jnp.where(incl, m1[:, n2:, n2:], 0.0)
    vs = stk(v)

    pm = _bmm_nt(lhs[:, :n2], sp) + _bmm(ak, vs)
    x = ab
    for i in range(6):
        pm = pm + _bmm(x, pm.astype(BF16))
        if i < 5:
            x = _bmm(x, x).astype(BF16)
    pmb = pm.astype(BF16)
    y2 = _bmm_nt(lhs[:, n2:], sp) + _bmm(rb, pmb) + _bmm(rk, vs)
    y = y2[:, :C] + y2[:, C:]

    upd_l = jnp.concatenate([pmb, vs], axis=1)
    upd_r = jnp.concatenate([stk(bt * g_last), stk(kt * g_last)], axis=1)
    s_scr[...] = st * g_c + _bmm_tn(upd_l, upd_r)

    mean = seg(y) * (1.0 / RWKV_HEAD)
    yc = y - mean
    var = seg(yc * yc) * (1.0 / RWKV_HEAD)
    y = yc * lax.rsqrt(var + RWKV_GN_EPS) * per_pair(lng_ref) + per_pair(lnb_ref) + bonus * v

    y = _merge_lanes(y, RWKV_PAIRS)
    orw = _dot((y * g).astype(BF16), wo_ref[...])
    mix = mg_ref[...].reshape(rows, D_MODEL) + jax.nn.sigmoid(gate_ref[...].reshape(rows, D_MODEL)) * orw
    out = x_ref[...].reshape(rows, D_MODEL) + _dot(mix.astype(BF16), wout_ref[...])
    o_ref[...] = out.reshape(nb, C, D_MODEL)


def _rwkv(p3, mg, x, prm, nb):
    batch, lp, _ = p3.shape
    n_chunks = lp // CHUNK
    real = lambda b, c: (b, jnp.maximum(c - 1, 0), 0)
    pspec = lambda w, col: pl.BlockSpec((nb, CHUNK, w), lambda b, c: (b, c, col // w))
    full = lambda shape: pl.BlockSpec(shape, lambda b, c: (0,) * len(shape))
    pairvec = full((RWKV_PAIRS, 1, LANES))
    vec = full((1, D_MODEL))
    shift = lambda w: pltpu.VMEM((nb, CHUNK + SUBLANES, w), F32)
    return pl.pallas_call(
        _rwkv_body,
        grid=(batch // nb, n_chunks),
        in_specs=[
            pspec(D_MODEL, COL_RR), pspec(D_MODEL, COL_RK), pspec(D_MODEL, COL_RV),
            pspec(SMALL_W, COL_SMALL), pspec(D_MODEL, COL_GATE_RWKV),
            pl.BlockSpec((nb, CHUNK, D_MODEL), real), pl.BlockSpec((nb, CHUNK, D_MODEL), real),
            vec, vec, vec, full((1, SMALL_W)),
            full((LANES, D_MODEL)), vec, full((LANES, D_MODEL)), vec, full((2 * LANES, D_MODEL)),
            pairvec, pairvec, pairvec, pairvec, pairvec,
            full((D_MODEL, D_MODEL)), full((D_MODEL, D_MODEL)),
        ],
        out_specs=pl.BlockSpec((nb, CHUNK, D_MODEL), real),
        out_shape=jax.ShapeDtypeStruct(x.shape, F32),
        scratch_shapes=[shift(D_MODEL), shift(D_MODEL), shift(D_MODEL), shift(SMALL_W),
                        pltpu.VMEM((nb * RWKV_PAIRS, LANES, LANES), F32)],
        compiler_params=pltpu.CompilerParams(
            dimension_semantics=("parallel", "arbitrary"), vmem_limit_bytes=VMEM_LIMIT),
        name="rwkv",
    )(p3, p3, p3, p3, p3, mg, x, *prm)


def _route_body(h_ref, g_ref, wr_ref, br_ref, idx_ref, prob_ref, rank_ref, cnt_ref, base_scr):
    tm = h_ref.shape[0]

    @pl.when(pl.program_id(0) == 0)
    def _():
        base_scr[...] = jnp.zeros_like(base_scr)

    h = h_ref[...]
    u = h * lax.rsqrt(jnp.mean(h * h, axis=-1, keepdims=True) + NORM_EPS) * g_ref[...]
    hi, mid, lo = _split3(u)
    ph = _dot(hi, wr_ref[...])
    pm = _dot(mid, wr_ref[:, :2 * LANES])
    pl_ = _dot(lo, wr_ref[:, :LANES])
    logits = (ph[:, :LANES] + (ph[:, LANES:2 * LANES] + pm[:, :LANES])
              + (ph[:, 2 * LANES:] + pm[:, LANES:] + pl_)) + br_ref[...]
    lane = lax.broadcasted_iota(jnp.int32, (tm, LANES), 1)
    logits = jnp.where(lane < N_EXPERTS, logits, -jnp.inf)
    idx_out = jnp.zeros((tm, LANES), jnp.int32)
    prob_out = jnp.zeros((tm, LANES), F32)
    denom = jnp.zeros((tm, 1), F32)
    hits = []
    top = None
    for k in range(TOP_K):
        m = jnp.max(logits, axis=-1, keepdims=True)
        idx = jnp.min(jnp.where(logits == m, lane, LANES), axis=-1, keepdims=True)
        hit = lane == idx
        top = m if top is None else top
        w = jnp.exp(m - top)
        idx_out = jnp.where(lane == k, idx, idx_out)
        prob_out = jnp.where(lane == k, w, prob_out)
        denom = denom + w
        hits.append(hit)
        logits = jnp.where(hit, -jnp.inf, logits)
    idx_ref[...] = idx_out
    prob_ref[...] = prob_out / denom

    tot = jnp.zeros((tm, LANES), F32)
    for hit in hits:
        tot = tot + jnp.where(hit, 1.0, 0.0)
    r = lax.broadcasted_iota(jnp.int32, (tm, tm), 0)
    c = lax.broadcasted_iota(jnp.int32, (tm, tm), 1)
    before = _dot((r > c).astype(BF16), tot.astype(BF16)) + base_scr[...]
    rank_out = jnp.zeros((tm, LANES), F32)
    for k, hit in enumerate(hits):
        rk = jnp.sum(jnp.where(hit, before, 0.0), axis=-1, keepdims=True)
        rank_out = jnp.where(lane == k, rk, rank_out)
    rank_ref[...] = rank_out.astype(jnp.int32)
    base_scr[...] += jnp.sum(tot, axis=0, keepdims=True)
    cnt_ref[...] = base_scr[...]


def _route(h2, g, wr3, br, tm):
    t = h2.shape[0]
    full = lambda shape: pl.BlockSpec(shape, lambda i: (0,) * len(shape))
    tile = lambda w: pl.BlockSpec((tm, w), lambda i: (i, 0))
    return pl.pallas_call(
        _route_body,
        grid=(t // tm,),
        in_specs=[tile(D_MODEL), full((1, D_MODEL)), full((D_MODEL, 3 * LANES)), full((1, LANES))],
        out_specs=[tile(LANES), tile(LANES), tile(LANES), full((1, LANES))],
        out_shape=[jax.ShapeDtypeStruct((t, LANES), jnp.int32),
                   jax.ShapeDtypeStruct((t, LANES), F32),
                   jax.ShapeDtypeStruct((t, LANES), jnp.int32),
                   jax.ShapeDtypeStruct((1, LANES), F32)],
        scratch_shapes=[pltpu.VMEM((1, LANES), F32)],
        compiler_params=pltpu.CompilerParams(
            dimension_semantics=("arbitrary",), vmem_limit_bytes=VMEM_LIMIT),
        name="route",
    )(h2, g, wr3, br)


def _dispatch_body(tail_ref, dest_ref, h_ref, xs_ref, hs_scr, sem):
    tm = h_ref.shape[0]
    zrows = hs_scr.shape[0]

    @pl.when(pl.program_id(0) == 0)
    def _():
        hs_scr[...] = jnp.zeros_like(hs_scr)
        for e in range(N_EXPERTS):
            start = pl.multiple_of(tail_ref[e] * SUBLANES, SUBLANES)
            pltpu.make_async_copy(hs_scr, xs_ref.at[pl.ds(start, zrows)], sem).start()
        for e in range(N_EXPERTS):
            pltpu.make_async_copy(hs_scr, xs_ref.at[pl.ds(0, zrows)], sem).wait()

        def zero_tile(j, carry):
            start = pl.multiple_of(j * zrows, zrows)
            pltpu.make_async_copy(hs_scr, xs_ref.at[pl.ds(start, zrows)], sem).start()
            return carry

        def wait_tile(j, carry):
            pltpu.make_async_copy(hs_scr, xs_ref.at[pl.ds(0, zrows)], sem).wait()
            return carry

        n_tiles = xs_ref.shape[0] // zrows
        lax.fori_loop(tail_ref[N_EXPERTS], n_tiles, zero_tile, 0)
        lax.fori_loop(tail_ref[N_EXPERTS], n_tiles, wait_tile, 0)

    for c in range(SUBLANES):
        hs_scr[pl.ds(c, tm, stride=SUBLANES), :] = h_ref[:, c * LANES:(c + 1) * LANES]

    def issue(t, carry):
        src = hs_scr.at[pl.ds(pl.multiple_of(t * SUBLANES, SUBLANES), SUBLANES)]
        for k in range(TOP_K):
            d = pl.multiple_of(dest_ref[0, t * TOP_K + k] * SUBLANES, SUBLANES)
            pltpu.make_async_copy(src, xs_ref.at[pl.ds(d, SUBLANES)], sem).start(priority=k % 2)
        return carry

    lax.fori_loop(0, tm, issue, 0, unroll=DMA_UNROLL)
    for k in range(TOP_K):
        pltpu.make_async_copy(hs_scr, xs_ref.at[pl.ds(0, tm * SUBLANES)], sem).wait()


def _dispatch(tail, dest2, h2, ns, tm):
    t = h2.shape[0]
    return pl.pallas_call(
        _dispatch_body,
        grid_spec=pltpu.PrefetchScalarGridSpec(
            num_scalar_prefetch=1,
            grid=(t // tm,),
            in_specs=[pl.BlockSpec((None, 1, tm * TOP_K), lambda i, tl: (i, 0, 0), memory_space=pltpu.SMEM),
                      pl.BlockSpec((tm, D_MODEL), lambda i, tl: (i, 0))],
            out_specs=pl.BlockSpec(memory_space=pl.ANY),
            scratch_shapes=[pltpu.VMEM((tm * SUBLANES, LANES), F32), pltpu.SemaphoreType.DMA(())],
        ),
        out_shape=jax.ShapeDtypeStruct((ns * SUBLANES, LANES), F32),
        compiler_params=pltpu.CompilerParams(
            dimension_semantics=("arbitrary",), vmem_limit_bytes=VMEM_LIMIT),
        name="dispatch",
    )(tail, dest2, h2)


def _experts_body(te_ref, nu_ref, xs_ref, g_ref, wgu_ref, bgu_ref, wdn_ref, bdn_ref, ys_ref,
                  wgu_scr, wdn_scr):
    i = pl.program_id(0)
    tm = xs_ref.shape[0] // SUBLANES

    @pl.when((i == 0) | (te_ref[i] != te_ref[jnp.maximum(i - 1, 0)]))
    def _():
        wgu_scr[...] = wgu_ref[0].astype(BF16)
        wdn_scr[...] = wdn_ref[0].astype(BF16)

    @pl.when(i < nu_ref[0])
    def _():
        x = jnp.concatenate([xs_ref[pl.ds(c, tm, stride=SUBLANES), :] for c in range(SUBLANES)], axis=1)
        u = x * lax.rsqrt(jnp.mean(x * x, axis=-1, keepdims=True) + NORM_EPS) * g_ref[...]
        hgu = _dot(u.astype(BF16), wgu_scr[...]) + bgu_ref[0]
        d = wdn_scr.shape[0]
        gate = jnp.minimum(hgu[:, :d], SWIGLU_LIMIT)
        up = jnp.clip(hgu[:, d:], -SWIGLU_LIMIT, SWIGLU_LIMIT)
        act = (up + 1.0) * (gate * jax.nn.sigmoid(gate * SWIGLU_ALPHA))
        y = _dot(act.astype(BF16), wdn_scr[...]) + bdn_ref[0]
        for c in range(SUBLANES):
            ys_ref[pl.ds(c, tm, stride=SUBLANES), :] = y[:, c * LANES:(c + 1) * LANES]

    @pl.when(i >= nu_ref[0])
    def _():
        ys_ref[...] = jnp.zeros_like(ys_ref)


def _experts(tile_expert, n_used, xs, g, wgu, bgu, wdn, bdn, tm):
    n_tiles = xs.shape[0] // (tm * SUBLANES)
    wsel = lambda shape: pl.BlockSpec(shape, lambda i, te, nu: (te[i], 0, 0))
    used = lambda i, te, nu: (jnp.minimum(i, nu[0] - 1), 0)
    return pl.pallas_call(
        _experts_body,
        grid_spec=pltpu.PrefetchScalarGridSpec(
            num_scalar_prefetch=2,
            grid=(n_tiles,),
            in_specs=[pl.BlockSpec((tm * SUBLANES, LANES), used),
                      pl.BlockSpec((1, D_MODEL), lambda i, te, nu: (0, 0)),
                      wsel((1, D_MODEL, 2 * D_MODEL)), wsel((1, 1, 2 * D_MODEL)),
                      wsel((1, D_MODEL, D_MODEL)), wsel((1, 1, D_MODEL))],
            out_specs=pl.BlockSpec((tm * SUBLANES, LANES), lambda i, te, nu: (i, 0)),
            scratch_shapes=[pltpu.VMEM((D_MODEL, 2 * D_MODEL), BF16), pltpu.VMEM((D_MODEL, D_MODEL), BF16)],
        ),
        out_shape=jax.ShapeDtypeStruct(xs.shape, F32),
        compiler_params=pltpu.CompilerParams(
            dimension_semantics=("arbitrary",), vmem_limit_bytes=VMEM_LIMIT),
        name="experts",
    )(tile_expert, n_used, xs, g, wgu, bgu, wdn, bdn)


def _combine_body(dcur_ref, dnext_ref, prob_ref, h_ref, gf_ref, ys_ref, o_ref, ybuf, sem):
    i = pl.program_id(0)
    n = pl.num_programs(0)
    tm = h_ref.shape[0]
    slot = i % 2

    def gather(dref, s):
        def issue(t, carry):
            row = pl.multiple_of(t * SUBLANES, SUBLANES)
            for k in range(TOP_K):
                d = pl.multiple_of(dref[0, t * TOP_K + k] * SUBLANES, SUBLANES)
                pltpu.make_async_copy(ys_ref.at[pl.ds(d, SUBLANES)], ybuf.at[s, k, pl.ds(row, SUBLANES)],
                                      sem.at[s]).start(priority=k % 2)
            return carry
        lax.fori_loop(0, tm, issue, 0, unroll=DMA_UNROLL)

    @pl.when(i == 0)
    def _():
        gather(dcur_ref, 0)

    @pl.when(i + 1 < n)
    def _():
        gather(dnext_ref, 1 - slot)

    for k in range(TOP_K):
        pltpu.make_async_copy(ys_ref.at[pl.ds(0, tm * SUBLANES)], ybuf.at[slot, k], sem.at[slot]).wait()
    prob = prob_ref[...]
    cols = []
    for c in range(SUBLANES):
        acc = h_ref[:, c * LANES:(c + 1) * LANES]
        for k in range(TOP_K):
            acc = acc + prob[:, k:k + 1] * ybuf[slot, k, pl.ds(c, tm, stride=SUBLANES), :]
        cols.append(acc)
    h = jnp.concatenate(cols, axis=1)
    o_ref[...] = h * lax.rsqrt(jnp.mean(h * h, axis=-1, keepdims=True) + NORM_EPS) * gf_ref[...]


def _combine(dest2, prob, h2, gf, ys, tm):
    t = h2.shape[0]
    n = t // tm
    smem = lambda imap: pl.BlockSpec((None, 1, tm * TOP_K), imap, memory_space=pltpu.SMEM)
    return pl.pallas_call(
        _combine_body,
        grid=(n,),
        in_specs=[smem(lambda i: (i, 0, 0)), smem(lambda i: (jnp.minimum(i + 1, n - 1), 0, 0)),
                  pl.BlockSpec((tm, LANES), lambda i: (i, 0)),
                  pl.BlockSpec((tm, D_MODEL), lambda i: (i, 0)),
                  pl.BlockSpec((1, D_MODEL), lambda i: (0, 0)),
                  pl.BlockSpec(memory_space=pl.ANY)],
        out_specs=pl.BlockSpec((tm, D_MODEL), lambda i: (i, 0)),
        out_shape=jax.ShapeDtypeStruct((t, D_MODEL), F32),
        scratch_shapes=[pltpu.VMEM((2, TOP_K, tm * SUBLANES, LANES), F32), pltpu.SemaphoreType.DMA((2,))],
        compiler_params=pltpu.CompilerParams(
            dimension_semantics=("arbitrary",), vmem_limit_bytes=VMEM_LIMIT),
        name="combine",
    )(dest2, dest2, prob, h2, gf, ys)


def _moe(h2, g, wr3, br, wgu, bgu, wdn, bdn, gf):
    t = h2.shape[0]
    tm_x = _pick_tile(t, (512, 256, 128))
    tm_c = _pick_tile(t, (256, 128))
    idx, prob, rank, cnt = _route(h2, g, wr3, br, tm_x)

    counts = cnt[0, :N_EXPERTS].astype(jnp.int32)
    padded = (counts + tm_x - 1) // tm_x * tm_x
    ends = jnp.cumsum(padded)
    offs = ends - padded
    ns = t * TOP_K + (N_EXPERTS + 1) * tm_x
    e_ids = jnp.arange(N_EXPERTS, dtype=jnp.int32)
    idx4 = idx[:, :TOP_K]
    dest = rank[:, :TOP_K] + jnp.sum(jnp.where(idx4[..., None] == e_ids, offs, 0), axis=-1)
    tile_start = jnp.arange(ns // tm_x, dtype=jnp.int32) * tm_x
    tile_expert = jnp.minimum(jnp.sum((tile_start[:, None] >= ends[None, :]).astype(jnp.int32), axis=1),
                              N_EXPERTS - 1)
    n_used = (ends[-1] // tm_x).reshape(1)

    xs = _dispatch(jnp.concatenate([offs + counts, n_used]), dest.reshape(t // tm_x, 1, tm_x * TOP_K), h2, ns, tm_x)
    ys = _experts(tile_expert, n_used, xs, g, wgu, bgu, wdn, bdn, tm_x)
    return _combine(dest.reshape(t // tm_c, 1, tm_c * TOP_K), prob, h2, gf, ys, tm_c)


def _pick_tile(n, prefs):
    for t in prefs:
        if n % t == 0:
            return t
    return n


def _pad_rows(w, rows, offset=0):
    out = jnp.zeros((rows, w.shape[1]), w.dtype)
    return out.at[offset:offset + w.shape[0]].set(w)


def kernel(x, meta_tokens, norm_mix_g, w_in, w_gla_a2, b_gla_a, gla_norm_g, w_gla_o, mu_r, mu_k, mu_v, mu_w, mu_a, mu_g, w_decay2, b_decay, w_a2, b_a, w_gate2, k_k, k_a, r_k, ln_x_g, ln_x_b, w_rwkv_o, w_out, norm_ffn_g, w_router, b_router, w_exp_gu, b_exp_gu, w_exp_down, b_exp_down, norm_final_g):
    batch, seq, d = x.shape
    assert d == D_MODEL and seq % CHUNK == 0 and w_in.shape[0] == 1
    lp = FRONT_PAD + N_META + seq
    row = lambda a: a.reshape(1, -1)

    meta = jnp.broadcast_to(meta_tokens[None].astype(x.dtype), (batch, N_META, d))
    hp = jnp.concatenate([jnp.zeros((batch, FRONT_PAD, d), x.dtype), meta, x], axis=1).reshape(batch * lp, d)

    splits = (GLA_KEY, GLA_KEY, D_MODEL, D_MODEL, GLA_GATE_RANK, D_MODEL, D_MODEL, D_MODEL,
              RWKV_DECAY_RANK, RWKV_A_RANK, RWKV_GATE_RANK, D_MODEL, D_MODEL)
    offs = [0]
    for s in splits:
        offs.append(offs[-1] + s)
    piece = lambda i: w_in[0][:, offs[i]:offs[i + 1]]
    zcols = lambda n: jnp.zeros((d, n), w_in.dtype)
    small = jnp.concatenate([piece(4), zcols(LANES - GLA_GATE_RANK), piece(8), piece(9), piece(10),
                             zcols(2 * LANES - RWKV_GATE_RANK)], axis=1)
    w_big = jnp.concatenate([piece(0), piece(1), piece(2), piece(3), piece(5), piece(6), piece(7),
                             piece(11), piece(12), small], axis=1).astype(BF16)

    m = batch * lp
    tm = _pick_tile(m, (1280, 640, 512, 320, 256, 192, 128, 64))
    tn = _pick_tile(NP_COLS, (2176, 512))
    p3 = _inproj(hp, row(norm_mix_g[0]), w_big, tm, tn).reshape(batch, lp, NP_COLS)

    mg = _gla(p3, _pad_rows(w_gla_a2[0], LANES).astype(BF16), row(b_gla_a[0]), row(gla_norm_g[0]),
              w_gla_o[0].astype(BF16), _pick_tile(batch, (GLA_SEQS_PER_STEP, 2, 1)))

    pv = lambda a: a.reshape(RWKV_PAIRS, 1, LANES)
    mu_small = jnp.concatenate([jnp.zeros((LANES,), F32), mu_w[0], mu_a[0], mu_g[0],
                                jnp.zeros((2 * LANES - RWKV_GATE_RANK,), F32)])
    prm = (row(mu_r[0]), row(mu_k[0]), row(mu_v[0]), row(mu_small),
           _pad_rows(w_decay2[0], LANES).astype(BF16), row(b_decay[0]),
           _pad_rows(w_a2[0], LANES, RWKV_DECAY_RANK).astype(BF16), row(b_a[0]),
           _pad_rows(w_gate2[0], 2 * LANES).astype(BF16),
           pv(k_k[0]), pv(k_a[0]), pv(r_k[0]), pv(ln_x_g[0]), pv(ln_x_b[0]),
           w_rwkv_o[0].astype(BF16), w_out[0].astype(BF16))
    h2 = _rwkv(p3, mg, x, prm, _pick_tile(batch, (RWKV_SEQS_PER_STEP, 1))).reshape(batch * seq, d)

    wr = jnp.zeros((d, LANES), F32).at[:, :N_EXPERTS].set(w_router[0])
    wr3 = jnp.concatenate(_split3(wr), axis=1)
    br = jnp.zeros((1, LANES), F32).at[0, :N_EXPERTS].set(b_router[0])
    out = _moe(h2, row(norm_ffn_g[0]), wr3, br, w_exp_gu[0], b_exp_gu[0][:, None, :],
               w_exp_down[0], b_exp_down[0][:, None, :], row(norm_final_g))
    return out.reshape(batch, seq, d)
```

```python
import math

import jax
import jax.numpy as jnp
from jax import lax
from jax.experimental import pallas as pl
from jax.experimental.pallas import tpu as pltpu

F32 = jnp.float32
BF16 = jnp.bfloat16

D_MODEL = 1024
N_META = 16
NORM_EPS = 1e-5
CHUNK = 64
FRONT_PAD = (-N_META) % CHUNK
GLA_HEADS = 4
GLA_DK = 128
GLA_DV = 256
GLA_KEY = GLA_HEADS * GLA_DK
GLA_GATE_RANK = 16
GLA_GATE_NORM = 16.0
RWKV_HEAD = 64
RWKV_PAIRS = D_MODEL // (2 * RWKV_HEAD)
RWKV_DECAY_RANK = 64
RWKV_A_RANK = 64
RWKV_GATE_RANK = 160
RWKV_GN_EPS = 64e-5
RWKV_DECAY_SCALE = math.exp(-0.5)
N_EXPERTS = 32
TOP_K = 4
SWIGLU_LIMIT = 7.0
SWIGLU_ALPHA = 1.702
GLA_SEQS_PER_STEP = 4
RWKV_SEQS_PER_STEP = 4
DMA_UNROLL = 8
LANES = 128
SUBLANES = 8
SMALL_W = 512
COL_GQ, COL_GK, COL_GV, COL_GR = 0, 512, 1024, 2048
COL_RR, COL_RK, COL_RV = 3072, 4096, 5120
COL_GATE_GLA, COL_GATE_RWKV, COL_SMALL = 6144, 7168, 8192
NP_COLS = COL_SMALL + SMALL_W
VMEM_LIMIT = 56 * 1024 * 1024


def _dot(a, b):
    return jnp.dot(a, b, preferred_element_type=F32)


def _bmm(a, b):
    return jnp.einsum("gik,gkj->gij", a, b, preferred_element_type=F32)


def _bmm_nt(a, b):
    return jnp.einsum("gik,gjk->gij", a, b, preferred_element_type=F32)


def _bmm_tn(a, b):
    return jnp.einsum("gti,gtj->gij", a, b, preferred_element_type=F32)


def _split2(x):
    hi = x.astype(BF16)
    lo = (x - hi.astype(F32)).astype(BF16)
    return hi, lo


def _split3(x):
    hi = x.astype(BF16)
    r1 = x - hi.astype(F32)
    mid = r1.astype(BF16)
    lo = (r1 - mid.astype(F32)).astype(BF16)
    return hi, mid, lo


def _cumsum_rows(x, tri, split):
    parts = split(x)
    out = _dot(tri, parts[0])
    for p in parts[1:]:
        out = out + _dot(tri, p)
    return out


def _softplus(x):
    return jnp.maximum(x, 0.0) + jnp.log1p(jnp.exp(-jnp.abs(x)))


def _tri_blocks(rows, blk):
    r = lax.broadcasted_iota(jnp.int32, (rows, rows), 0)
    c = lax.broadcasted_iota(jnp.int32, (rows, rows), 1)
    return ((r >= c) & (r // blk == c // blk)).astype(BF16)


def _split_lanes(x, n, w):
    s, c, _ = x.shape
    return jnp.stack([x[:, :, j * w:(j + 1) * w] for j in range(n)], axis=1).reshape(s * n, c, w)


def _merge_lanes(x, n):
    sn, c, w = x.shape
    x = x.reshape(sn // n, n, c, w)
    return jnp.concatenate([x[:, j] for j in range(n)], axis=-1).reshape(sn // n * c, n * w)


def _inproj_body(x_ref, g_ref, w_ref, o_ref, u_scr):
    @pl.when(pl.program_id(1) == 0)
    def _():
        x = x_ref[...]
        ms = jnp.mean(x * x, axis=-1, keepdims=True)
        u_scr[...] = (x * lax.rsqrt(ms + NORM_EPS) * g_ref[...]).astype(BF16)

    o_ref[...] = _dot(u_scr[...], w_ref[...])


def _inproj(hp, g, w, tm, tn):
    m = hp.shape[0]
    return pl.pallas_call(
        _inproj_body,
        grid=(m // tm, NP_COLS // tn),
        in_specs=[
            pl.BlockSpec((tm, D_MODEL), lambda i, j: (i, 0)),
            pl.BlockSpec((1, D_MODEL), lambda i, j: (0, 0)),
            pl.BlockSpec((D_MODEL, tn), lambda i, j: (0, j)),
        ],
        out_specs=pl.BlockSpec((tm, tn), lambda i, j: (i, j)),
        out_shape=jax.ShapeDtypeStruct((m, NP_COLS), F32),
        scratch_shapes=[pltpu.VMEM((tm, D_MODEL), BF16)],
        compiler_params=pltpu.CompilerParams(
            dimension_semantics=("parallel", "arbitrary"), vmem_limit_bytes=VMEM_LIMIT),
        name="inproj",
    )(hp, g, w)


def _gla_body(q_ref, k_ref, v_ref, r_ref, sm_ref, gate_ref, wa2_ref, ba_ref, ng_ref, wo_ref,
              o_ref, s_scr):
    c = pl.program_id(1)
    nb = q_ref.shape[0]
    C = CHUNK
    rows = nb * C
    H = GLA_HEADS

    @pl.when(c == 0)
    def _():
        s_scr[...] = jnp.zeros_like(s_scr)

    row = lax.broadcasted_iota(jnp.int32, (rows, 1), 0) % C
    causal = (lax.broadcasted_iota(jnp.int32, (C, C), 0) >= lax.broadcasted_iota(jnp.int32, (C, C), 1))[None]

    z = _dot(sm_ref[...].reshape(rows, SMALL_W)[:, 0:LANES].astype(BF16), wa2_ref[...]) + ba_ref[...]
    gk = -_softplus(-z) * (1.0 / GLA_GATE_NORM)
    gk = jnp.where((c > 0) | (row >= FRONT_PAD), gk, 0.0)
    b = _cumsum_rows(gk, _tri_blocks(rows, C), _split3).reshape(nb, C, GLA_KEY)
    b_ref = b[:, C // 2 - 1:C // 2, :]
    b_last = b[:, C - 1:C, :]

    q = q_ref[...] * (GLA_DK ** -0.5)
    k = k_ref[...]
    keys = lambda x: _split_lanes(x, H, GLA_DK)
    vals = lambda x: _split_lanes(x, H, GLA_DV)
    qe = keys((q * jnp.exp(b - b_ref)).astype(BF16))
    ke = keys((k * jnp.exp(b_ref - b)).astype(BF16))
    qs = keys((q * jnp.exp(b)).astype(BF16))
    kd = keys((k * jnp.exp(b_last - b)).astype(BF16))
    decay = keys(jnp.exp(b_last))
    v = vals(v_ref[...].astype(BF16))
    silu_r = r_ref[...]
    silu_r = vals(silu_r * jax.nn.sigmoid(silu_r))

    a = jnp.where(causal, _bmm_nt(qe, ke), 0.0)
    st = s_scr[...]
    o = _bmm(a.astype(BF16), v) + _bmm_nt(qs, st.astype(BF16))
    s_scr[...] = st * decay + _bmm_tn(v, kd)
    o = o * lax.rsqrt(jnp.mean(o * o, axis=-1, keepdims=True) + NORM_EPS) * ng_ref[...]
    og = _merge_lanes(o * silu_r, H).astype(BF16)
    out = jax.nn.sigmoid(gate_ref[...].reshape(rows, D_MODEL)) * _dot(og, wo_ref[...])
    o_ref[...] = out.reshape(nb, C, D_MODEL)


def _gla(p3, wa2, ba, ng, wo, nb):
    batch, lp, _ = p3.shape
    n_chunks = lp // CHUNK
    pspec = lambda w, col: pl.BlockSpec((nb, CHUNK, w), lambda b, c: (b, c, col // w))
    full = lambda shape: pl.BlockSpec(shape, lambda b, c: (0,) * len(shape))
    return pl.pallas_call(
        _gla_body,
        grid=(batch // nb, n_chunks),
        in_specs=[
            pspec(GLA_KEY, COL_GQ), pspec(GLA_KEY, COL_GK), pspec(D_MODEL, COL_GV),
            pspec(D_MODEL, COL_GR), pspec(SMALL_W, COL_SMALL), pspec(D_MODEL, COL_GATE_GLA),
            full((LANES, GLA_KEY)), full((1, GLA_KEY)), full((1, GLA_DV)), full((D_MODEL, D_MODEL)),
        ],
        out_specs=pl.BlockSpec((nb, CHUNK, D_MODEL), lambda b, c: (b, jnp.maximum(c - 1, 0), 0)),
        out_shape=jax.ShapeDtypeStruct((batch, (n_chunks - 1) * CHUNK, D_MODEL), F32),
        scratch_shapes=[pltpu.VMEM((nb * GLA_HEADS, GLA_DV, GLA_DK), F32)],
        compiler_params=pltpu.CompilerParams(
            dimension_semantics=("parallel", "arbitrary"), vmem_limit_bytes=VMEM_LIMIT),
        name="gla",
    )(p3, p3, p3, p3, p3, p3, wa2, ba, ng, wo)


def _stack_heads(x, lane_lo):
    zero = jnp.zeros((), x.dtype)
    return jnp.concatenate([jnp.where(lane_lo, x, zero), jnp.where(lane_lo, zero, x)], axis=1)


def _rwkv_body(pr_ref, pk_ref, pv_ref, sm_ref, gate_ref, mg_ref, x_ref,
               mur_ref, muk_ref, muv_ref, musm_ref, wd_ref, bd_ref, wa_ref, ba_ref, wg_ref,
               kk_ref, ka_ref, rk_ref, lng_ref, lnb_ref, wo_ref, wout_ref,
               o_ref,
               shr_scr, shk_scr, shv_scr, shs_scr, s_scr):
    c = pl.program_id(1)
    nb = pr_ref.shape[0]
    C = CHUNK
    rows = nb * C
    G = nb * RWKV_PAIRS

    @pl.when(c == 0)
    def _():
        s_scr[...] = jnp.zeros_like(s_scr)
        for scr in (shr_scr, shk_scr, shv_scr, shs_scr):
            scr[:, SUBLANES - 1:SUBLANES, :] = jnp.zeros((nb, 1, scr.shape[2]), F32)

    def lerp(x_ref_, scr, mu_ref):
        x = x_ref_[...]
        scr[:, SUBLANES:SUBLANES + C, :] = x
        prev = scr[:, SUBLANES - 1:SUBLANES - 1 + C, :]
        scr[:, SUBLANES - 1:SUBLANES, :] = x[:, C - 1:C, :]
        return x + (prev - x) * mu_ref[...]

    pairs = lambda x: _split_lanes(x, RWKV_PAIRS, LANES)
    per_pair = lambda ref: jnp.broadcast_to(ref[...][None], (nb, RWKV_PAIRS, 1, LANES)).reshape(G, 1, LANES)

    r = lerp(pr_ref, shr_scr, mur_ref)
    k = lerp(pk_ref, shk_scr, muk_ref)
    v = lerp(pv_ref, shv_scr, muv_ref)
    ls = lerp(sm_ref, shs_scr, musm_ref).reshape(rows, SMALL_W)
    s1 = ls[:, LANES:2 * LANES]
    xw = bd_ref[...] + _dot(jnp.tanh(s1).astype(BF16), wd_ref[...])
    logw = -RWKV_DECAY_SCALE * jax.nn.sigmoid(xw)
    a = jax.nn.sigmoid(ba_ref[...] + _dot(s1.astype(BF16), wa_ref[...]))
    g = _dot(jax.nn.sigmoid(ls[:, 2 * LANES:4 * LANES]).astype(BF16), wg_ref[...])

    cl = _cumsum_rows(logw, _tri_blocks(rows, C), _split2).reshape(nb, C, D_MODEL)
    logw = logw.reshape(nb, C, D_MODEL)
    cref = cl[:, C // 2 - 1:C // 2, :]
    clast = cl[:, C - 1:C, :]
    e_neg = pairs(jnp.exp(cref - cl))
    e_prev = pairs(jnp.exp(cl - logw - cref))
    e_cur = pairs(jnp.exp(cl - cref))
    g_last = pairs(jnp.exp(clast - cref))
    g_ref = pairs(jnp.exp(cref))
    g_c = pairs(jnp.exp(clast))
    r = pairs(r)
    k = pairs(k)
    v = pairs(v)
    a = pairs(a.reshape(nb, C, D_MODEL))

    lane_lo = lax.broadcasted_iota(jnp.int32, (1, C, LANES), 2) < RWKV_HEAD
    er = lax.broadcasted_iota(jnp.int32, (LANES, LANES), 0)
    ec = lax.broadcasted_iota(jnp.int32, (LANES, LANES), 1)
    seg_ones = ((er // RWKV_HEAD) == (ec // RWKV_HEAD)).astype(BF16)
    strict = ((er % C) > (ec % C))[None]
    incl = ((er % C) >= (ec % C))[None]

    def seg(x):
        return _dot(x.reshape(G * C, LANES).astype(BF16), seg_ones).reshape(G, C, LANES)

    kk = k * per_pair(kk_ref)
    kk = kk * lax.rsqrt(jnp.maximum(seg(kk * kk), 1e-24))
    k2 = k * (1.0 + (a - 1.0) * per_pair(ka_ref))
    bonus = seg(r * k2 * per_pair(rk_ref))
    at = -kk * e_prev
    rt = r * e_cur
    bt = kk * a * e_neg
    kt = k2 * e_neg
    st = s_scr[...]
    sp = (st * g_ref).astype(BF16)

    stk = lambda z: _stack_heads(z.astype(BF16), lane_lo)
    lhs = jnp.concatenate([stk(at), stk(rt)], axis=1)
    rhs = jnp.concatenate([stk(bt), stk(kt)], axis=1)
    m1 = _bmm_nt(lhs, rhs).astype(BF16)
    n2 = 2 * C
    ab = jnp.where(strict, m1[:, :n2, :n2], 0.0)
    ak = jnp.where(strict, m1[:, :n2, n2:], 0.0)
    rb = jnp.where(incl, m1[:, n2:, :n2], 0.0)
    rk = jnp.where(incl, m1[:, n2:, n2:], 0.0)
    vs = stk(v)

    pm = _bmm_nt(lhs[:, :n2], sp) + _bmm(ak, vs)
    x = ab
    for i in range(6):
        pm = pm + _bmm(x, pm.astype(BF16))
        if i < 5:
            x = _bmm(x, x).astype(BF16)
    pmb = pm.astype(BF16)
    y2 = _bmm_nt(lhs[:, n2:], sp) + _bmm(rb, pmb) + _bmm(rk, vs)
    y = y2[:, :C] + y2[:, C:]

    upd_l = jnp.concatenate([pmb, vs], axis=1)
    upd_r = jnp.concatenate([stk(bt * g_last), stk(kt * g_last)], axis=1)
    s_scr[...] = st * g_c + _bmm_tn(upd_l, upd_r)

    mean = seg(y) * (1.0 / RWKV_HEAD)
    yc = y - mean
    var = seg(yc * yc) * (1.0 / RWKV_HEAD)
    y = yc * lax.rsqrt(var + RWKV_GN_EPS) * per_pair(lng_ref) + per_pair(lnb_ref) + bonus * v

    y = _merge_lanes(y, RWKV_PAIRS)
    orw = _dot((y * g).astype(BF16), wo_ref[...])
    mix = mg_ref[...].reshape(rows, D_MODEL) + jax.nn.sigmoid(gate_ref[...].reshape(rows, D_MODEL)) * orw
    out = x_ref[...].reshape(rows, D_MODEL) + _dot(mix.astype(BF16), wout_ref[...])
    o_ref[...] = out.reshape(nb, C, D_MODEL)


def _rwkv(p3, mg, x, prm, nb):
    batch, lp, _ = p3.shape
    n_chunks = lp // CHUNK
    real = lambda b, c: (b, jnp.maximum(c - 1, 0), 0)
    pspec = lambda w, col: pl.BlockSpec((nb, CHUNK, w), lambda b, c: (b, c, col // w))
    full = lambda shape: pl.BlockSpec(shape, lambda b, c: (0,) * len(shape))
    pairvec = full((RWKV_PAIRS, 1, LANES))
    vec = full((1, D_MODEL))
    shift = lambda w: pltpu.VMEM((nb, CHUNK + SUBLANES, w), F32)
    return pl.pallas_call(
        _rwkv_body,
        grid=(batch // nb, n_chunks),
        in_specs=[
            pspec(D_MODEL, COL_RR), pspec(D_MODEL, COL_RK), pspec(D_MODEL, COL_RV),
            pspec(SMALL_W, COL_SMALL), pspec(D_MODEL, COL_GATE_RWKV),
            pl.BlockSpec((nb, CHUNK, D_MODEL), real), pl.BlockSpec((nb, CHUNK, D_MODEL), real),
            vec, vec, vec, full((1, SMALL_W)),
            full((LANES, D_MODEL)), vec, full((LANES, D_MODEL)), vec, full((2 * LANES, D_MODEL)),
            pairvec, pairvec, pairvec, pairvec, pairvec,
            full((D_MODEL, D_MODEL)), full((D_MODEL, D_MODEL)),
        ],
        out_specs=pl.BlockSpec((nb, CHUNK, D_MODEL), real),
        out_shape=jax.ShapeDtypeStruct(x.shape, F32),
        scratch_shapes=[shift(D_MODEL), shift(D_MODEL), shift(D_MODEL), shift(SMALL_W),
                        pltpu.VMEM((nb * RWKV_PAIRS, LANES, LANES), F32)],
        compiler_params=pltpu.CompilerParams(
            dimension_semantics=("parallel", "arbitrary"), vmem_limit_bytes=VMEM_LIMIT),
        name="rwkv",
    )(p3, p3, p3, p3, p3, mg, x, *prm)


def _route_body(h_ref, g_ref, wr_ref, br_ref, idx_ref, prob_ref, rank_ref, cnt_ref, base_scr):
    tm = h_ref.shape[0]

    @pl.when(pl.program_id(0) == 0)
    def _():
        base_scr[...] = jnp.zeros_like(base_scr)

    h = h_ref[...]
    u = h * lax.rsqrt(jnp.mean(h * h, axis=-1, keepdims=True) + NORM_EPS) * g_ref[...]
    hi, mid, lo = _split3(u)
    ph = _dot(hi, wr_ref[...])
    pm = _dot(mid, wr_ref[:, :2 * LANES])
    pl_ = _dot(lo, wr_ref[:, :LANES])
    logits = (ph[:, :LANES] + (ph[:, LANES:2 * LANES] + pm[:, :LANES])
              + (ph[:, 2 * LANES:] + pm[:, LANES:] + pl_)) + br_ref[...]
    lane = lax.broadcasted_iota(jnp.int32, (tm, LANES), 1)
    logits = jnp.where(lane < N_EXPERTS, logits, -jnp.inf)
    idx_out = jnp.zeros((tm, LANES), jnp.int32)
    prob_out = jnp.zeros((tm, LANES), F32)
    denom = jnp.zeros((tm, 1), F32)
    hits = []
    top = None
    for k in range(TOP_K):
        m = jnp.max(logits, axis=-1, keepdims=True)
        idx = jnp.min(jnp.where(logits == m, lane, LANES), axis=-1, keepdims=True)
        hit = lane == idx
        top = m if top is None else top
        w = jnp.exp(m - top)
        idx_out = jnp.where(lane == k, idx, idx_out)
        prob_out = jnp.where(lane == k, w, prob_out)
        denom = denom + w
        hits.append(hit)
        logits = jnp.where(hit, -jnp.inf, logits)
    idx_ref[...] = idx_out
    prob_ref[...] = prob_out / denom

    tot = jnp.zeros((tm, LANES), F32)
    for hit in hits:
        tot = tot + jnp.where(hit, 1.0, 0.0)
    r = lax.broadcasted_iota(jnp.int32, (tm, tm), 0)
    c = lax.broadcasted_iota(jnp.int32, (tm, tm), 1)
    before = _dot((r > c).astype(BF16), tot.astype(BF16)) + base_scr[...]
    rank_out = jnp.zeros((tm, LANES), F32)
    for k, hit in enumerate(hits):
        rk = jnp.sum(jnp.where(hit, before, 0.0), axis=-1, keepdims=True)
        rank_out = jnp.where(lane == k, rk, rank_out)
    rank_ref[...] = rank_out.astype(jnp.int32)
    base_scr[...] += jnp.sum(tot, axis=0, keepdims=True)
    cnt_ref[...] = base_scr[...]


def _route(h2, g, wr3, br, tm):
    t = h2.shape[0]
    full = lambda shape: pl.BlockSpec(shape, lambda i: (0,) * len(shape))
    tile = lambda w: pl.BlockSpec((tm, w), lambda i: (i, 0))
    return pl.pallas_call(
        _route_body,
        grid=(t // tm,),
        in_specs=[tile(D_MODEL), full((1, D_MODEL)), full((D_MODEL, 3 * LANES)), full((1, LANES))],
        out_specs=[tile(LANES), tile(LANES), tile(LANES), full((1, LANES))],
        out_shape=[jax.ShapeDtypeStruct((t, LANES), jnp.int32),
                   jax.ShapeDtypeStruct((t, LANES), F32),
                   jax.ShapeDtypeStruct((t, LANES), jnp.int32),
                   jax.ShapeDtypeStruct((1, LANES), F32)],
        scratch_shapes=[pltpu.VMEM((1, LANES), F32)],
        compiler_params=pltpu.CompilerParams(
            dimension_semantics=("arbitrary",), vmem_limit_bytes=VMEM_LIMIT),
        name="route",
    )(h2, g, wr3, br)


def _dispatch_body(tail_ref, dest_ref, h_ref, xs_ref, hs_scr, sem):
    tm = h_ref.shape[0]
    zrows = hs_scr.shape[0]

    @pl.when(pl.program_id(0) == 0)
    def _():
        hs_scr[...] = jnp.zeros_like(hs_scr)
        for e in range(N_EXPERTS):
            start = pl.multiple_of(tail_ref[e] * SUBLANES, SUBLANES)
            pltpu.make_async_copy(hs_scr, xs_ref.at[pl.ds(start, zrows)], sem).start()
        for e in range(N_EXPERTS):
            pltpu.make_async_copy(hs_scr, xs_ref.at[pl.ds(0, zrows)], sem).wait()

        def zero_tile(j, carry):
            start = pl.multiple_of(j * zrows, zrows)
            pltpu.make_async_copy(hs_scr, xs_ref.at[pl.ds(start, zrows)], sem).start()
            return carry

        def wait_tile(j, carry):
            pltpu.make_async_copy(hs_scr, xs_ref.at[pl.ds(0, zrows)], sem).wait()
            return carry

        n_tiles = xs_ref.shape[0] // zrows
        lax.fori_loop(tail_ref[N_EXPERTS], n_tiles, zero_tile, 0)
        lax.fori_loop(tail_ref[N_EXPERTS], n_tiles, wait_tile, 0)

    for c in range(SUBLANES):
        hs_scr[pl.ds(c, tm, stride=SUBLANES), :] = h_ref[:, c * LANES:(c + 1) * LANES]

    def issue(t, carry):
        src = hs_scr.at[pl.ds(pl.multiple_of(t * SUBLANES, SUBLANES), SUBLANES)]
        for k in range(TOP_K):
            d = pl.multiple_of(dest_ref[0, t * TOP_K + k] * SUBLANES, SUBLANES)
            pltpu.make_async_copy(src, xs_ref.at[pl.ds(d, SUBLANES)], sem).start(priority=k % 2)
        return carry

    lax.fori_loop(0, tm, issue, 0, unroll=DMA_UNROLL)
    for k in range(TOP_K):
        pltpu.make_async_copy(hs_scr, xs_ref.at[pl.ds(0, tm * SUBLANES)], sem).wait()


def _dispatch(tail, dest2, h2, ns, tm):
    t = h2.shape[0]
    return pl.pallas_call(
        _dispatch_body,
        grid_spec=pltpu.PrefetchScalarGridSpec(
            num_scalar_prefetch=1,
            grid=(t // tm,),
            in_specs=[pl.BlockSpec((None, 1, tm * TOP_K), lambda i, tl: (i, 0, 0), memory_space=pltpu.SMEM),
                      pl.BlockSpec((tm, D_MODEL), lambda i, tl: (i, 0))],
            out_specs=pl.BlockSpec(memory_space=pl.ANY),
            scratch_shapes=[pltpu.VMEM((tm * SUBLANES, LANES), F32), pltpu.SemaphoreType.DMA(())],
        ),
        out_shape=jax.ShapeDtypeStruct((ns * SUBLANES, LANES), F32),
        compiler_params=pltpu.CompilerParams(
            dimension_semantics=("arbitrary",), vmem_limit_bytes=VMEM_LIMIT),
        name="dispatch",
    )(tail, dest2, h2)


def _experts_body(te_ref, nu_ref, xs_ref, g_ref, wgu_ref, bgu_ref, wdn_ref, bdn_ref, ys_ref,
                  wgu_scr, wdn_scr):
    i = pl.program_id(0)
    tm = xs_ref.shape[0] // SUBLANES

    @pl.when((i == 0) | (te_ref[i] != te_ref[jnp.maximum(i - 1, 0)]))
    def _():
        wgu_scr[...] = wgu_ref[0].astype(BF16)
        wdn_scr[...] = wdn_ref[0].astype(BF16)

    @pl.when(i < nu_ref[0])
    def _():
        x = jnp.concatenate([xs_ref[pl.ds(c, tm, stride=SUBLANES), :] for c in range(SUBLANES)], axis=1)
        u = x * lax.rsqrt(jnp.mean(x * x, axis=-1, keepdims=True) + NORM_EPS) * g_ref[...]
        hgu = _dot(u.astype(BF16), wgu_scr[...]) + bgu_ref[0]
        d = wdn_scr.shape[0]
        gate = jnp.minimum(hgu[:, :d], SWIGLU_LIMIT)
        up = jnp.clip(hgu[:, d:], -SWIGLU_LIMIT, SWIGLU_LIMIT)
        act = (up + 1.0) * (gate * jax.nn.sigmoid(gate * SWIGLU_ALPHA))
        y = _dot(act.astype(BF16), wdn_scr[...]) + bdn_ref[0]
        for c in range(SUBLANES):
            ys_ref[pl.ds(c, tm, stride=SUBLANES), :] = y[:, c * LANES:(c + 1) * LANES]

    @pl.when(i >= nu_ref[0])
    def _():
        ys_ref[...] = jnp.zeros_like(ys_ref)


def _experts(tile_expert, n_used, xs, g, wgu, bgu, wdn, bdn, tm):
    n_tiles = xs.shape[0] // (tm * SUBLANES)
    wsel = lambda shape: pl.BlockSpec(shape, lambda i, te, nu: (te[i], 0, 0))
    used = lambda i, te, nu: (jnp.minimum(i, nu[0] - 1), 0)
    return pl.pallas_call(
        _experts_body,
        grid_spec=pltpu.PrefetchScalarGridSpec(
            num_scalar_prefetch=2,
            grid=(n_tiles,),
            in_specs=[pl.BlockSpec((tm * SUBLANES, LANES), used),
                      pl.BlockSpec((1, D_MODEL), lambda i, te, nu: (0, 0)),
                      wsel((1, D_MODEL, 2 * D_MODEL)), wsel((1, 1, 2 * D_MODEL)),
                      wsel((1, D_MODEL, D_MODEL)), wsel((1, 1, D_MODEL))],
            out_specs=pl.BlockSpec((tm * SUBLANES, LANES), lambda i, te, nu: (i, 0)),
            scratch_shapes=[pltpu.VMEM((D_MODEL, 2 * D_MODEL), BF16), pltpu.VMEM((D_MODEL, D_MODEL), BF16)],
        ),
        out_shape=jax.ShapeDtypeStruct(xs.shape, F32),
        compiler_params=pltpu.CompilerParams(
            dimension_semantics=("arbitrary",), vmem_limit_bytes=VMEM_LIMIT),
        name="experts",
    )(tile_expert, n_used, xs, g, wgu, bgu, wdn, bdn)


def _combine_body(dcur_ref, dnext_ref, prob_ref, h_ref, gf_ref, ys_ref, o_ref, ybuf, sem):
    i = pl.program_id(0)
    n = pl.num_programs(0)
    tm = h_ref.shape[0]
    slot = i % 2

    def gather(dref, s):
        def issue(t, carry):
            row = pl.multiple_of(t * SUBLANES, SUBLANES)
            for k in range(TOP_K):
                d = pl.multiple_of(dref[0, t * TOP_K + k] * SUBLANES, SUBLANES)
                pltpu.make_async_copy(ys_ref.at[pl.ds(d, SUBLANES)], ybuf.at[s, k, pl.ds(row, SUBLANES)],
                                      sem.at[s]).start(priority=k % 2)
            return carry
        lax.fori_loop(0, tm, issue, 0, unroll=DMA_UNROLL)

    @pl.when(i == 0)
    def _():
        gather(dcur_ref, 0)

    @pl.when(i + 1 < n)
    def _():
        gather(dnext_ref, 1 - slot)

    for k in range(TOP_K):
        pltpu.make_async_copy(ys_ref.at[pl.ds(0, tm * SUBLANES)], ybuf.at[slot, k], sem.at[slot]).wait()
    prob = prob_ref[...]
    cols = []
    for c in range(SUBLANES):
        acc = h_ref[:, c * LANES:(c + 1) * LANES]
        for k in range(TOP_K):
            acc = acc + prob[:, k:k + 1] * ybuf[slot, k, pl.ds(c, tm, stride=SUBLANES), :]
        cols.append(acc)
    h = jnp.concatenate(cols, axis=1)
    o_ref[...] = h * lax.rsqrt(jnp.mean(h * h, axis=-1, keepdims=True) + NORM_EPS) * gf_ref[...]


def _combine(dest2, prob, h2, gf, ys, tm):
    t = h2.shape[0]
    n = t // tm
    smem = lambda imap: pl.BlockSpec((None, 1, tm * TOP_K), imap, memory_space=pltpu.SMEM)
    return pl.pallas_call(
        _combine_body,
        grid=(n,),
        in_specs=[smem(lambda i: (i, 0, 0)), smem(lambda i: (jnp.minimum(i + 1, n - 1), 0, 0)),
                  pl.BlockSpec((tm, LANES), lambda i: (i, 0)),
                  pl.BlockSpec((tm, D_MODEL), lambda i: (i, 0)),
                  pl.BlockSpec((1, D_MODEL), lambda i: (0, 0)),
                  pl.BlockSpec(memory_space=pl.ANY)],
        out_specs=pl.BlockSpec((tm, D_MODEL), lambda i: (i, 0)),
        out_shape=jax.ShapeDtypeStruct((t, D_MODEL), F32),
        scratch_shapes=[pltpu.VMEM((2, TOP_K, tm * SUBLANES, LANES), F32), pltpu.SemaphoreType.DMA((2,))],
        compiler_params=pltpu.CompilerParams(
            dimension_semantics=("arbitrary",), vmem_limit_bytes=VMEM_LIMIT),
        name="combine",
    )(dest2, dest2, prob, h2, gf, ys)


def _moe(h2, g, wr3, br, wgu, bgu, wdn, bdn, gf):
    t = h2.shape[0]
    tm_x = _pick_tile(t, (512, 256, 128))
    tm_c = _pick_tile(t, (256, 128))
    idx, prob, rank, cnt = _route(h2, g, wr3, br, tm_x)

    counts = cnt[0, :N_EXPERTS].astype(jnp.int32)
    padded = (counts + tm_x - 1) // tm_x * tm_x
    ends = jnp.cumsum(padded)
    offs = ends - padded
    ns = t * TOP_K + (N_EXPERTS + 1) * tm_x
    e_ids = jnp.arange(N_EXPERTS, dtype=jnp.int32)
    idx4 = idx[:, :TOP_K]
    dest = rank[:, :TOP_K] + jnp.sum(jnp.where(idx4[..., None] == e_ids, offs, 0), axis=-1)
    tile_start = jnp.arange(ns // tm_x, dtype=jnp.int32) * tm_x
    tile_expert = jnp.minimum(jnp.sum((tile_start[:, None] >= ends[None, :]).astype(jnp.int32), axis=1),
                              N_EXPERTS - 1)
    n_used = (ends[-1] // tm_x).reshape(1)

    xs = _dispatch(jnp.concatenate([offs + counts, n_used]), dest.reshape(t // tm_x, 1, tm_x * TOP_K), h2, ns, tm_x)
    ys = _experts(tile_expert, n_used, xs, g, wgu, bgu, wdn, bdn, tm_x)
    return _combine(dest.reshape(t // tm_c, 1, tm_c * TOP_K), prob, h2, gf, ys, tm_c)


def _pick_tile(n, prefs):
    for t in prefs:
        if n % t == 0:
            return t
    return n


def _pad_rows(w, rows, offset=0):
    out = jnp.zeros((rows, w.shape[1]), w.dtype)
    return out.at[offset:offset + w.shape[0]].set(w)


def kernel(x, meta_tokens, norm_mix_g, w_in, w_gla_a2, b_gla_a, gla_norm_g, w_gla_o, mu_r, mu_k, mu_v, mu_w, mu_a, mu_g, w_decay2, b_decay, w_a2, b_a, w_gate2, k_k, k_a, r_k, ln_x_g, ln_x_b, w_rwkv_o, w_out, norm_ffn_g, w_router, b_router, w_exp_gu, b_exp_gu, w_exp_down, b_exp_down, norm_final_g):
    batch, seq, d = x.shape
    assert d == D_MODEL and seq % CHUNK == 0 and w_in.shape[0] == 1
    lp = FRONT_PAD + N_META + seq
    row = lambda a: a.reshape(1, -1)

    meta = jnp.broadcast_to(meta_tokens[None].astype(x.dtype), (batch, N_META, d))
    hp = jnp.concatenate([jnp.zeros((batch, FRONT_PAD, d), x.dtype), meta, x], axis=1).reshape(batch * lp, d)

    splits = (GLA_KEY, GLA_KEY, D_MODEL, D_MODEL, GLA_GATE_RANK, D_MODEL, D_MODEL, D_MODEL,
              RWKV_DECAY_RANK, RWKV_A_RANK, RWKV_GATE_RANK, D_MODEL, D_MODEL)
    offs = [0]
    for s in splits:
        offs.append(offs[-1] + s)
    piece = lambda i: w_in[0][:, offs[i]:offs[i + 1]]
    zcols = lambda n: jnp.zeros((d, n), w_in.dtype)
    small = jnp.concatenate([piece(4), zcols(LANES - GLA_GATE_RANK), piece(8), piece(9), piece(10),
                             zcols(2 * LANES - RWKV_GATE_RANK)], axis=1)
    w_big = jnp.concatenate([piece(0), piece(1), piece(2), piece(3), piece(5), piece(6), piece(7),
                             piece(11), piece(12), small], axis=1).astype(BF16)

    m = batch * lp
    tm = _pick_tile(m, (1280, 640, 512, 320, 256, 192, 128, 64))
    tn = _pick_tile(NP_COLS, (2176, 512))
    p3 = _inproj(hp, row(norm_mix_g[0]), w_big, tm, tn).reshape(batch, lp, NP_COLS)

    mg = _gla(p3, _pad_rows(w_gla_a2[0], LANES).astype(BF16), row(b_gla_a[0]), row(gla_norm_g[0]),
              w_gla_o[0].astype(BF16), _pick_tile(batch, (GLA_SEQS_PER_STEP, 2, 1)))

    pv = lambda a: a.reshape(RWKV_PAIRS, 1, LANES)
    mu_small = jnp.concatenate([jnp.zeros((LANES,), F32), mu_w[0], mu_a[0], mu_g[0],
                                jnp.zeros((2 * LANES - RWKV_GATE_RANK,), F32)])
    prm = (row(mu_r[0]), row(mu_k[0]), row(mu_v[0]), row(mu_small),
           _pad_rows(w_decay2[0], LANES).astype(BF16), row(b_decay[0]),
           _pad_rows(w_a2[0], LANES, RWKV_DECAY_RANK).astype(BF16), row(b_a[0]),
           _pad_rows(w_gate2[0], 2 * LANES).astype(BF16),
           pv(k_k[0]), pv(k_a[0]), pv(r_k[0]), pv(ln_x_g[0]), pv(ln_x_b[0]),
           w_rwkv_o[0].astype(BF16), w_out[0].astype(BF16))
    h2 = _rwkv(p3, mg, x, prm, _pick_tile(batch, (RWKV_SEQS_PER_STEP, 1))).reshape(batch * seq, d)

    wr = jnp.zeros((d, LANES), F32).at[:, :N_EXPERTS].set(w_router[0])
    wr3 = jnp.concatenate(_split3(wr), axis=1)
    br = jnp.zeros((1, LANES), F32).at[0, :N_EXPERTS].set(b_router[0])
    out = _moe(h2, row(norm_ffn_g[0]), wr3, br, w_exp_gu[0], b_exp_gu[0][:, None, :],
               w_exp_down[0], b_exp_down[0][:, None, :], row(norm_final_g))
    return out.reshape(batch, seq, d)
```

```python
import math

import jax
import jax.numpy as jnp
from jax import lax
from jax.experimental import pallas as pl
from jax.experimental.pallas import tpu as pltpu

F32 = jnp.float32
BF16 = jnp.bfloat16

D_MODEL = 1024
N_META = 16
NORM_EPS = 1e-5
CHUNK = 64
FRONT_PAD = (-N_META) % CHUNK
GLA_HEADS = 4
GLA_DK = 128
GLA_DV = 256
GLA_KEY = GLA_HEADS * GLA_DK
GLA_GATE_RANK = 16
GLA_GATE_NORM = 16.0
RWKV_HEAD = 64
RWKV_PAIRS = D_MODEL // (2 * RWKV_HEAD)
RWKV_DECAY_RANK = 64
RWKV_A_RANK = 64
RWKV_GATE_RANK = 160
RWKV_GN_EPS = 64e-5
RWKV_DECAY_SCALE = math.exp(-0.5)
N_EXPERTS = 32
TOP_K = 4
SWIGLU_LIMIT = 7.0
SWIGLU_ALPHA = 1.702
GLA_SEQS_PER_STEP = 8
RWKV_SEQS_PER_STEP = 4
DMA_UNROLL = 8
LANES = 128
SUBLANES = 8
SMALL_W = 512
COL_GQ, COL_GK, COL_GV, COL_GR = 0, 512, 1024, 2048
COL_RR, COL_RK, COL_RV = 3072, 4096, 5120
COL_GATE_GLA, COL_GATE_RWKV, COL_SMALL = 6144, 7168, 8192
NP_COLS = COL_SMALL + SMALL_W
VMEM_LIMIT = 56 * 1024 * 1024


def _dot(a, b):
    return jnp.dot(a, b, preferred_element_type=F32)


def _bmm(a, b):
    return jnp.einsum("gik,gkj->gij", a, b, preferred_element_type=F32)


def _bmm_nt(a, b):
    return jnp.einsum("gik,gjk->gij", a, b, preferred_element_type=F32)


def _bmm_tn(a, b):
    return jnp.einsum("gti,gtj->gij", a, b, preferred_element_type=F32)


def _split2(x):
    hi = x.astype(BF16)
    lo = (x - hi.astype(F32)).astype(BF16)
    return hi, lo


def _split3(x):
    hi = x.astype(BF16)
    r1 = x - hi.astype(F32)
    mid = r1.astype(BF16)
    lo = (r1 - mid.astype(F32)).astype(BF16)
    return hi, mid, lo


def _cumsum_rows(x, tri, split):
    parts = split(x)
    out = _dot(tri, parts[0])
    for p in parts[1:]:
        out = out + _dot(tri, p)
    return out


def _softplus(x):
    return jnp.maximum(x, 0.0) + jnp.log1p(jnp.exp(-jnp.abs(x)))


def _tri_blocks(rows, blk):
    r = lax.broadcasted_iota(jnp.int32, (rows, rows), 0)
    c = lax.broadcasted_iota(jnp.int32, (rows, rows), 1)
    return ((r >= c) & (r // blk == c // blk)).astype(BF16)


def _split_lanes(x, n, w):
    s, c, _ = x.shape
    return jnp.stack([x[:, :, j * w:(j + 1) * w] for j in range(n)], axis=1).reshape(s * n, c, w)


def _merge_lanes(x, n):
    sn, c, w = x.shape
    x = x.reshape(sn // n, n, c, w)
    return jnp.concatenate([x[:, j] for j in range(n)], axis=-1).reshape(sn // n * c, n * w)


def _inproj_body(x_ref, g_ref, w_ref, o_ref, u_scr):
    @pl.when(pl.program_id(1) == 0)
    def _():
        x = x_ref[...]
        ms = jnp.mean(x * x, axis=-1, keepdims=True)
        u_scr[...] = (x * lax.rsqrt(ms + NORM_EPS) * g_ref[...]).astype(BF16)

    o_ref[...] = _dot(u_scr[...], w_ref[...])


def _inproj(hp, g, w, tm, tn):
    m = hp.shape[0]
    return pl.pallas_call(
        _inproj_body,
        grid=(m // tm, NP_COLS // tn),
        in_specs=[
            pl.BlockSpec((tm, D_MODEL), lambda i, j: (i, 0)),
            pl.BlockSpec((1, D_MODEL), lambda i, j: (0, 0)),
            pl.BlockSpec((D_MODEL, tn), lambda i, j: (0, j)),
        ],
        out_specs=pl.BlockSpec((tm, tn), lambda i, j: (i, j)),
        out_shape=jax.ShapeDtypeStruct((m, NP_COLS), F32),
        scratch_shapes=[pltpu.VMEM((tm, D_MODEL), BF16)],
        compiler_params=pltpu.CompilerParams(
            dimension_semantics=("parallel", "arbitrary"), vmem_limit_bytes=VMEM_LIMIT),
        name="inproj",
    )(hp, g, w)


def _gla_body(q_ref, k_ref, v_ref, r_ref, sm_ref, gate_ref, wa2_ref, ba_ref, ng_ref, wo_ref,
              o_ref, s_scr):
    c = pl.program_id(1)
    nb = q_ref.shape[0]
    C = CHUNK
    rows = nb * C
    H = GLA_HEADS

    @pl.when(c == 0)
    def _():
        s_scr[...] = jnp.zeros_like(s_scr)

    row = lax.broadcasted_iota(jnp.int32, (rows, 1), 0) % C
    causal = (lax.broadcasted_iota(jnp.int32, (C, C), 0) >= lax.broadcasted_iota(jnp.int32, (C, C), 1))[None]

    z = _dot(sm_ref[...].reshape(rows, SMALL_W)[:, 0:LANES].astype(BF16), wa2_ref[...]) + ba_ref[...]
    gk = -_softplus(-z) * (1.0 / GLA_GATE_NORM)
    gk = jnp.where((c > 0) | (row >= FRONT_PAD), gk, 0.0)
    b = _cumsum_rows(gk, _tri_blocks(rows, C), _split3).reshape(nb, C, GLA_KEY)
    b_ref = b[:, C // 2 - 1:C // 2, :]
    b_last = b[:, C - 1:C, :]

    q = q_ref[...] * (GLA_DK ** -0.5)
    k = k_ref[...]
    keys = lambda x: _split_lanes(x, H, GLA_DK)
    vals = lambda x: _split_lanes(x, H, GLA_DV)
    qe = keys((q * jnp.exp(b - b_ref)).astype(BF16))
    ke = keys((k * jnp.exp(b_ref - b)).astype(BF16))
    qs = keys((q * jnp.exp(b)).astype(BF16))
    kd = keys((k * jnp.exp(b_last - b)).astype(BF16))
    decay = keys(jnp.exp(b_last))
    v = vals(v_ref[...].astype(BF16))
    silu_r = r_ref[...]
    silu_r = vals(silu_r * jax.nn.sigmoid(silu_r))

    a = jnp.where(causal, _bmm_nt(qe, ke), 0.0)
    st = s_scr[...]
    o = _bmm(a.astype(BF16), v) + _bmm_nt(qs, st.astype(BF16))
    s_scr[...] = st * decay + _bmm_tn(v, kd)
    o = o * lax.rsqrt(jnp.mean(o * o, axis=-1, keepdims=True) + NORM_EPS) * ng_ref[...]
    og = _merge_lanes(o * silu_r, H).astype(BF16)
    out = jax.nn.sigmoid(gate_ref[...].reshape(rows, D_MODEL)) * _dot(og, wo_ref[...])
    o_ref[...] = out.reshape(nb, C, D_MODEL)


def _gla(p3, wa2, ba, ng, wo, nb):
    batch, lp, _ = p3.shape
    n_chunks = lp // CHUNK
    pspec = lambda w, col: pl.BlockSpec((nb, CHUNK, w), lambda b, c: (b, c, col // w))
    full = lambda shape: pl.BlockSpec(shape, lambda b, c: (0,) * len(shape))
    return pl.pallas_call(
        _gla_body,
        grid=(batch // nb, n_chunks),
        in_specs=[
            pspec(GLA_KEY, COL_GQ), pspec(GLA_KEY, COL_GK), pspec(D_MODEL, COL_GV),
            pspec(D_MODEL, COL_GR), pspec(SMALL_W, COL_SMALL), pspec(D_MODEL, COL_GATE_GLA),
            full((LANES, GLA_KEY)), full((1, GLA_KEY)), full((1, GLA_DV)), full((D_MODEL, D_MODEL)),
        ],
        out_specs=pl.BlockSpec((nb, CHUNK, D_MODEL), lambda b, c: (b, jnp.maximum(c - 1, 0), 0)),
        out_shape=jax.ShapeDtypeStruct((batch, (n_chunks - 1) * CHUNK, D_MODEL), F32),
        scratch_shapes=[pltpu.VMEM((nb * GLA_HEADS, GLA_DV, GLA_DK), F32)],
        compiler_params=pltpu.CompilerParams(
            dimension_semantics=("parallel", "arbitrary"), vmem_limit_bytes=VMEM_LIMIT),
        name="gla",
    )(p3, p3, p3, p3, p3, p3, wa2, ba, ng, wo)


def _stack_heads(x, lane_lo):
    zero = jnp.zeros((), x.dtype)
    return jnp.concatenate([jnp.where(lane_lo, x, zero), jnp.where(lane_lo, zero, x)], axis=1)


def _rwkv_body(pr_ref, pk_ref, pv_ref, sm_ref, gate_ref, mg_ref, x_ref,
               mur_ref, muk_ref, muv_ref, musm_ref, wd_ref, bd_ref, wa_ref, ba_ref, wg_ref,
               kk_ref, ka_ref, rk_ref, lng_ref, lnb_ref, wo_ref, wout_ref,
               o_ref,
               shr_scr, shk_scr, shv_scr, shs_scr, s_scr):
    c = pl.program_id(1)
    nb = pr_ref.shape[0]
    C = CHUNK
    rows = nb * C
    G = nb * RWKV_PAIRS

    @pl.when(c == 0)
    def _():
        s_scr[...] = jnp.zeros_like(s_scr)
        for scr in (shr_scr, shk_scr, shv_scr, shs_scr):
            scr[:, SUBLANES - 1:SUBLANES, :] = jnp.zeros((nb, 1, scr.shape[2]), F32)

    def lerp(x_ref_, scr, mu_ref):
        x = x_ref_[...]
        scr[:, SUBLANES:SUBLANES + C, :] = x
        prev = scr[:, SUBLANES - 1:SUBLANES - 1 + C, :]
        scr[:, SUBLANES - 1:SUBLANES, :] = x[:, C - 1:C, :]
        return x + (prev - x) * mu_ref[...]

    pairs = lambda x: _split_lanes(x, RWKV_PAIRS, LANES)
    per_pair = lambda ref: jnp.broadcast_to(ref[...][None], (nb, RWKV_PAIRS, 1, LANES)).reshape(G, 1, LANES)

    r = lerp(pr_ref, shr_scr, mur_ref)
    k = lerp(pk_ref, shk_scr, muk_ref)
    v = lerp(pv_ref, shv_scr, muv_ref)
    ls = lerp(sm_ref, shs_scr, musm_ref).reshape(rows, SMALL_W)
    s1 = ls[:, LANES:2 * LANES]
    xw = bd_ref[...] + _dot(jnp.tanh(s1).astype(BF16), wd_ref[...])
    logw = -RWKV_DECAY_SCALE * jax.nn.sigmoid(xw)
    a = jax.nn.sigmoid(ba_ref[...] + _dot(s1.astype(BF16), wa_ref[...]))
    g = _dot(jax.nn.sigmoid(ls[:, 2 * LANES:4 * LANES]).astype(BF16), wg_ref[...])

    cl = _cumsum_rows(logw, _tri_blocks(rows, C), _split2).reshape(nb, C, D_MODEL)
    logw = logw.reshape(nb, C, D_MODEL)
    cref = cl[:, C // 2 - 1:C // 2, :]
    clast = cl[:, C - 1:C, :]
    e_neg = pairs(jnp.exp(cref - cl))
    e_prev = pairs(jnp.exp(cl - logw - cref))
    e_cur = pairs(jnp.exp(cl - cref))
    g_last = pairs(jnp.exp(clast - cref))
    g_ref = pairs(jnp.exp(cref))
    g_c = pairs(jnp.exp(clast))
    r = pairs(r)
    k = pairs(k)
    v = pairs(v)
    a = pairs(a.reshape(nb, C, D_MODEL))

    lane_lo = lax.broadcasted_iota(jnp.int32, (1, C, LANES), 2) < RWKV_HEAD
    er = lax.broadcasted_iota(jnp.int32, (LANES, LANES), 0)
    ec = lax.broadcasted_iota(jnp.int32, (LANES, LANES), 1)
    seg_ones = ((er // RWKV_HEAD) == (ec // RWKV_HEAD)).astype(BF16)
    strict = ((er % C) > (ec % C))[None]
    incl = ((er % C) >= (ec % C))[None]

    def seg(x):
        return _dot(x.reshape(G * C, LANES).astype(BF16), seg_ones).reshape(G, C, LANES)

    kk = k * per_pair(kk_ref)
    kk = kk * lax.rsqrt(jnp.maximum(seg(kk * kk), 1e-24))
    k2 = k * (1.0 + (a - 1.0) * per_pair(ka_ref))
    bonus = seg(r * k2 * per_pair(rk_ref))
    at = -kk * e_prev
    rt = r * e_cur
    bt = kk * a * e_neg
    kt = k2 * e_neg
    st = s_scr[...]
    sp = (st * g_ref).astype(BF16)

    stk = lambda z: _stack_heads(z.astype(BF16), lane_lo)
    lhs = jnp.concatenate([stk(at), stk(rt)], axis=1)
    rhs = jnp.concatenate([stk(bt), stk(kt)], axis=1)
    m1 = _bmm_nt(lhs, rhs).astype(BF16)
    n2 = 2 * C
    ab = jnp.where(strict, m1[:, :n2, :n2], 0.0)
    ak = jnp.where(strict, m1[:, :n2, n2:], 0.0)
    rb = jnp.where(incl, m1[:, n2:, :n2], 0.0)
    rk = jnp.where(incl, m1[:, n2:, n2:], 0.0)
    vs = stk(v)

    pm = _bmm_nt(lhs[:, :n2], sp) + _bmm(ak, vs)
    x = ab
    for i in range(6):
        pm = pm + _bmm(x, pm.astype(BF16))
        if i < 5:
            x = _bmm(x, x).astype(BF16)
    pmb = pm.astype(BF16)
    y2 = _bmm_nt(lhs[:, n2:], sp) + _bmm(rb, pmb) + _bmm(rk, vs)
    y = y2[:, :C] + y2[:, C:]

    upd_l = jnp.concatenate([pmb, vs], axis=1)
    upd_r = jnp.concatenate([stk(bt * g_last), stk(kt * g_last)], axis=1)
    s_scr[...] = st * g_c + _bmm_tn(upd_l, upd_r)

    mean = seg(y) * (1.0 / RWKV_HEAD)
    yc = y - mean
    var = seg(yc * yc) * (1.0 / RWKV_HEAD)
    y = yc * lax.rsqrt(var + RWKV_GN_EPS) * per_pair(lng_ref) + per_pair(lnb_ref) + bonus * v

    y = _merge_lanes(y, RWKV_PAIRS)
    orw = _dot((y * g).astype(BF16), wo_ref[...])
    mix = mg_ref[...].reshape(rows, D_MODEL) + jax.nn.sigmoid(gate_ref[...].reshape(rows, D_MODEL)) * orw
    out = x_ref[...].reshape(rows, D_MODEL) + _dot(mix.astype(BF16), wout_ref[...])
    o_ref[...] = out.reshape(nb, C, D_MODEL)


def _rwkv(p3, mg, x, prm, nb):
    batch, lp, _ = p3.shape
    n_chunks = lp // CHUNK
    real = lambda b, c: (b, jnp.maximum(c - 1, 0), 0)
    pspec = lambda w, col: pl.BlockSpec((nb, CHUNK, w), lambda b, c: (b, c, col // w))
    full = lambda shape: pl.BlockSpec(shape, lambda b, c: (0,) * len(shape))
    pairvec = full((RWKV_PAIRS, 1, LANES))
    vec = full((1, D_MODEL))
    shift = lambda w: pltpu.VMEM((nb, CHUNK + SUBLANES, w), F32)
    return pl.pallas_call(
        _rwkv_body,
        grid=(batch // nb, n_chunks),
        in_specs=[
            pspec(D_MODEL, COL_RR), pspec(D_MODEL, COL_RK), pspec(D_MODEL, COL_RV),
            pspec(SMALL_W, COL_SMALL), pspec(D_MODEL, COL_GATE_RWKV),
            pl.BlockSpec((nb, CHUNK, D_MODEL), real), pl.BlockSpec((nb, CHUNK, D_MODEL), real),
            vec, vec, vec, full((1, SMALL_W)),
            full((LANES, D_MODEL)), vec, full((LANES, D_MODEL)), vec, full((2 * LANES, D_MODEL)),
            pairvec, pairvec, pairvec, pairvec, pairvec,
            full((D_MODEL, D_MODEL)), full((D_MODEL, D_MODEL)),
        ],
        out_specs=pl.BlockSpec((nb, CHUNK, D_MODEL), real),
        out_shape=jax.ShapeDtypeStruct(x.shape, F32),
        scratch_shapes=[shift(D_MODEL), shift(D_MODEL), shift(D_MODEL), shift(SMALL_W),
                        pltpu.VMEM((nb * RWKV_PAIRS, LANES, LANES), F32)],
        compiler_params=pltpu.CompilerParams(
            dimension_semantics=("parallel", "arbitrary"), vmem_limit_bytes=VMEM_LIMIT),
        name="rwkv",
    )(p3, p3, p3, p3, p3, mg, x, *prm)


def _route_body(h_ref, g_ref, wr_ref, br_ref, idx_ref, prob_ref, rank_ref, cnt_ref, base_scr):
    tm = h_ref.shape[0]

    @pl.when(pl.program_id(0) == 0)
    def _():
        base_scr[...] = jnp.zeros_like(base_scr)

    h = h_ref[...]
    u = h * lax.rsqrt(jnp.mean(h * h, axis=-1, keepdims=True) + NORM_EPS) * g_ref[...]
    hi, mid, lo = _split3(u)
    ph = _dot(hi, wr_ref[...])
    pm = _dot(mid, wr_ref[:, :2 * LANES])
    pl_ = _dot(lo, wr_ref[:, :LANES])
    logits = (ph[:, :LANES] + (ph[:, LANES:2 * LANES] + pm[:, :LANES])
              + (ph[:, 2 * LANES:] + pm[:, LANES:] + pl_)) + br_ref[...]
    lane = lax.broadcasted_iota(jnp.int32, (tm, LANES), 1)
    logits = jnp.where(lane < N_EXPERTS, logits, -jnp.inf)
    idx_out = jnp.zeros((tm, LANES), jnp.int32)
    prob_out = jnp.zeros((tm, LANES), F32)
    denom = jnp.zeros((tm, 1), F32)
    hits = []
    top = None
    for k in range(TOP_K):
        m = jnp.max(logits, axis=-1, keepdims=True)
        idx = jnp.min(jnp.where(logits == m, lane, LANES), axis=-1, keepdims=True)
        hit = lane == idx
        top = m if top is None else top
        w = jnp.exp(m - top)
        idx_out = jnp.where(lane == k, idx, idx_out)
        prob_out = jnp.where(lane == k, w, prob_out)
        denom = denom + w
        hits.append(hit)
        logits = jnp.where(hit, -jnp.inf, logits)
    idx_ref[...] = idx_out
    prob_ref[...] = prob_out / denom

    tot = jnp.zeros((tm, LANES), F32)
    for hit in hits:
        tot = tot + jnp.where(hit, 1.0, 0.0)
    r = lax.broadcasted_iota(jnp.int32, (tm, tm), 0)
    c = lax.broadcasted_iota(jnp.int32, (tm, tm), 1)
    before = _dot((r > c).astype(BF16), tot.astype(BF16)) + base_scr[...]
    rank_out = jnp.zeros((tm, LANES), F32)
    for k, hit in enumerate(hits):
        rk = jnp.sum(jnp.where(hit, before, 0.0), axis=-1, keepdims=True)
        rank_out = jnp.where(lane == k, rk, rank_out)
    rank_ref[...] = rank_out.astype(jnp.int32)
    base_scr[...] += jnp.sum(tot, axis=0, keepdims=True)
    cnt_ref[...] = base_scr[...]


def _route(h2, g, wr3, br, tm):
    t = h2.shape[0]
    full = lambda shape: pl.BlockSpec(shape, lambda i: (0,) * len(shape))
    tile = lambda w: pl.BlockSpec((tm, w), lambda i: (i, 0))
    return pl.pallas_call(
        _route_body,
        grid=(t // tm,),
        in_specs=[tile(D_MODEL), full((1, D_MODEL)), full((D_MODEL, 3 * LANES)), full((1, LANES))],
        out_specs=[tile(LANES), tile(LANES), tile(LANES), full((1, LANES))],
        out_shape=[jax.ShapeDtypeStruct((t, LANES), jnp.int32),
                   jax.ShapeDtypeStruct((t, LANES), F32),
                   jax.ShapeDtypeStruct((t, LANES), jnp.int32),
                   jax.ShapeDtypeStruct((1, LANES), F32)],
        scratch_shapes=[pltpu.VMEM((1, LANES), F32)],
        compiler_params=pltpu.CompilerParams(
            dimension_semantics=("arbitrary",), vmem_limit_bytes=VMEM_LIMIT),
        name="route",
    )(h2, g, wr3, br)


def _dispatch_body(tail_ref, dest_ref, h_ref, xs_ref, hs_scr, sem):
    tm = h_ref.shape[0]
    zrows = hs_scr.shape[0]

    @pl.when(pl.program_id(0) == 0)
    def _():
        hs_scr[...] = jnp.zeros_like(hs_scr)
        for e in range(N_EXPERTS):
            start = pl.multiple_of(tail_ref[e] * SUBLANES, SUBLANES)
            pltpu.make_async_copy(hs_scr, xs_ref.at[pl.ds(start, zrows)], sem).start()
        for e in range(N_EXPERTS):
            pltpu.make_async_copy(hs_scr, xs_ref.at[pl.ds(0, zrows)], sem).wait()

        def zero_tile(j, carry):
            start = pl.multiple_of(j * zrows, zrows)
            pltpu.make_async_copy(hs_scr, xs_ref.at[pl.ds(start, zrows)], sem).start()
            return carry

        def wait_tile(j, carry):
            pltpu.make_async_copy(hs_scr, xs_ref.at[pl.ds(0, zrows)], sem).wait()
            return carry

        n_tiles = xs_ref.shape[0] // zrows
        lax.fori_loop(tail_ref[N_EXPERTS], n_tiles, zero_tile, 0)
        lax.fori_loop(tail_ref[N_EXPERTS], n_tiles, wait_tile, 0)

    for c in range(SUBLANES):
        hs_scr[pl.ds(c, tm, stride=SUBLANES), :] = h_ref[:, c * LANES:(c + 1) * LANES]

    def issue(t, carry):
        src = hs_scr.at[pl.ds(pl.multiple_of(t * SUBLANES, SUBLANES), SUBLANES)]
        for k in range(TOP_K):
            d = pl.multiple_of(dest_ref[0, t * TOP_K + k] * SUBLANES, SUBLANES)
            pltpu.make_async_copy(src, xs_ref.at[pl.ds(d, SUBLANES)], sem).start(priority=k % 2)
        return carry

    lax.fori_loop(0, tm, issue, 0, unroll=DMA_UNROLL)
    for k in range(TOP_K):
        pltpu.make_async_copy(hs_scr, xs_ref.at[pl.ds(0, tm * SUBLANES)], sem).wait()


def _dispatch(tail, dest2, h2, ns, tm):
    t = h2.shape[0]
    return pl.pallas_call(
        _dispatch_body,
        grid_spec=pltpu.PrefetchScalarGridSpec(
            num_scalar_prefetch=1,
            grid=(t // tm,),
            in_specs=[pl.BlockSpec((None, 1, tm * TOP_K), lambda i, tl: (i, 0, 0), memory_space=pltpu.SMEM),
                      pl.BlockSpec((tm, D_MODEL), lambda i, tl: (i, 0))],
            out_specs=pl.BlockSpec(memory_space=pl.ANY),
            scratch_shapes=[pltpu.VMEM((tm * SUBLANES, LANES), F32), pltpu.SemaphoreType.DMA(())],
        ),
        out_shape=jax.ShapeDtypeStruct((ns * SUBLANES, LANES), F32),
        compiler_params=pltpu.CompilerParams(
            dimension_semantics=("arbitrary",), vmem_limit_bytes=VMEM_LIMIT),
        name="dispatch",
    )(tail, dest2, h2)


def _experts_body(te_ref, nu_ref, xs_ref, g_ref, wgu_ref, bgu_ref, wdn_ref, bdn_ref, ys_ref,
                  wgu_scr, wdn_scr):
    i = pl.program_id(0)
    tm = xs_ref.shape[0] // SUBLANES

    @pl.when((i == 0) | (te_ref[i] != te_ref[jnp.maximum(i - 1, 0)]))
    def _():
        wgu_scr[...] = wgu_ref[0].astype(BF16)
        wdn_scr[...] = wdn_ref[0].astype(BF16)

    @pl.when(i < nu_ref[0])
    def _():
        x = jnp.concatenate([xs_ref[pl.ds(c, tm, stride=SUBLANES), :] for c in range(SUBLANES)], axis=1)
        u = x * lax.rsqrt(jnp.mean(x * x, axis=-1, keepdims=True) + NORM_EPS) * g_ref[...]
        hgu = _dot(u.astype(BF16), wgu_scr[...]) + bgu_ref[0]
        d = wdn_scr.shape[0]
        gate = jnp.minimum(hgu[:, :d], SWIGLU_LIMIT)
        up = jnp.clip(hgu[:, d:], -SWIGLU_LIMIT, SWIGLU_LIMIT)
        act = (up + 1.0) * (gate * jax.nn.sigmoid(gate * SWIGLU_ALPHA))
        y = _dot(act.astype(BF16), wdn_scr[...]) + bdn_ref[0]
        for c in range(SUBLANES):
            ys_ref[pl.ds(c, tm, stride=SUBLANES), :] = y[:, c * LANES:(c + 1) * LANES]

    @pl.when(i >= nu_ref[0])
    def _():
        ys_ref[...] = jnp.zeros_like(ys_ref)


def _experts(tile_expert, n_used, xs, g, wgu, bgu, wdn, bdn, tm):
    n_tiles = xs.shape[0] // (tm * SUBLANES)
    wsel = lambda shape: pl.BlockSpec(shape, lambda i, te, nu: (te[i], 0, 0))
    used = lambda i, te, nu: (jnp.minimum(i, nu[0] - 1), 0)
    return pl.pallas_call(
        _experts_body,
        grid_spec=pltpu.PrefetchScalarGridSpec(
            num_scalar_prefetch=2,
            grid=(n_tiles,),
            in_specs=[pl.BlockSpec((tm * SUBLANES, LANES), used),
                      pl.BlockSpec((1, D_MODEL), lambda i, te, nu: (0, 0)),
                      wsel((1, D_MODEL, 2 * D_MODEL)), wsel((1, 1, 2 * D_MODEL)),
                      wsel((1, D_MODEL, D_MODEL)), wsel((1, 1, D_MODEL))],
            out_specs=pl.BlockSpec((tm * SUBLANES, LANES), lambda i, te, nu: (i, 0)),
            scratch_shapes=[pltpu.VMEM((D_MODEL, 2 * D_MODEL), BF16), pltpu.VMEM((D_MODEL, D_MODEL), BF16)],
        ),
        out_shape=jax.ShapeDtypeStruct(xs.shape, F32),
        compiler_params=pltpu.CompilerParams(
            dimension_semantics=("arbitrary",), vmem_limit_bytes=VMEM_LIMIT),
        name="experts",
    )(tile_expert, n_used, xs, g, wgu, bgu, wdn, bdn)


def _combine_body(dcur_ref, dnext_ref, prob_ref, h_ref, gf_ref, ys_ref, o_ref, ybuf, sem):
    i = pl.program_id(0)
    n = pl.num_programs(0)
    tm = h_ref.shape[0]
    slot = i % 2

    def gather(dref, s):
        def issue(t, carry):
            row = pl.multiple_of(t * SUBLANES, SUBLANES)
            for k in range(TOP_K):
                d = pl.multiple_of(dref[0, t * TOP_K + k] * SUBLANES, SUBLANES)
                pltpu.make_async_copy(ys_ref.at[pl.ds(d, SUBLANES)], ybuf.at[s, k, pl.ds(row, SUBLANES)],
                                      sem.at[s]).start(priority=k % 2)
            return carry
        lax.fori_loop(0, tm, issue, 0, unroll=DMA_UNROLL)

    @pl.when(i == 0)
    def _():
        gather(dcur_ref, 0)

    @pl.when(i + 1 < n)
    def _():
        gather(dnext_ref, 1 - slot)

    for k in range(TOP_K):
        pltpu.make_async_copy(ys_ref.at[pl.ds(0, tm * SUBLANES)], ybuf.at[slot, k], sem.at[slot]).wait()
    prob = prob_ref[...]
    cols = []
    for c in range(SUBLANES):
        acc = h_ref[:, c * LANES:(c + 1) * LANES]
        for k in range(TOP_K):
            acc = acc + prob[:, k:k + 1] * ybuf[slot, k, pl.ds(c, tm, stride=SUBLANES), :]
        cols.append(acc)
    h = jnp.concatenate(cols, axis=1)
    o_ref[...] = h * lax.rsqrt(jnp.mean(h * h, axis=-1, keepdims=True) + NORM_EPS) * gf_ref[...]


def _combine(dest2, prob, h2, gf, ys, tm):
    t = h2.shape[0]
    n = t // tm
    smem = lambda imap: pl.BlockSpec((None, 1, tm * TOP_K), imap, memory_space=pltpu.SMEM)
    return pl.pallas_call(
        _combine_body,
        grid=(n,),
        in_specs=[smem(lambda i: (i, 0, 0)), smem(lambda i: (jnp.minimum(i + 1, n - 1), 0, 0)),
                  pl.BlockSpec((tm, LANES), lambda i: (i, 0)),
                  pl.BlockSpec((tm, D_MODEL), lambda i: (i, 0)),
                  pl.BlockSpec((1, D_MODEL), lambda i: (0, 0)),
                  pl.BlockSpec(memory_space=pl.ANY)],
        out_specs=pl.BlockSpec((tm, D_MODEL), lambda i: (i, 0)),
        out_shape=jax.ShapeDtypeStruct((t, D_MODEL), F32),
        scratch_shapes=[pltpu.VMEM((2, TOP_K, tm * SUBLANES, LANES), F32), pltpu.SemaphoreType.DMA((2,))],
        compiler_params=pltpu.CompilerParams(
            dimension_semantics=("arbitrary",), vmem_limit_bytes=VMEM_LIMIT),
        name="combine",
    )(dest2, dest2, prob, h2, gf, ys)


def _moe(h2, g, wr3, br, wgu, bgu, wdn, bdn, gf):
    t = h2.shape[0]
    tm_x = _pick_tile(t, (512, 256, 128))
    tm_c = _pick_tile(t, (512, 256, 128))
    idx, prob, rank, cnt = _route(h2, g, wr3, br, tm_x)

    counts = cnt[0, :N_EXPERTS].astype(jnp.int32)
    padded = (counts + tm_x - 1) // tm_x * tm_x
    ends = jnp.cumsum(padded)
    offs = ends - padded
    ns = t * TOP_K + (N_EXPERTS + 1) * tm_x
    e_ids = jnp.arange(N_EXPERTS, dtype=jnp.int32)
    idx4 = idx[:, :TOP_K]
    dest = rank[:, :TOP_K] + jnp.sum(jnp.where(idx4[..., None] == e_ids, offs, 0), axis=-1)
    tile_start = jnp.arange(ns // tm_x, dtype=jnp.int32) * tm_x
    tile_expert = jnp.minimum(jnp.sum((tile_start[:, None] >= ends[None, :]).astype(jnp.int32), axis=1),
                              N_EXPERTS - 1)
    n_used = (ends[-1] // tm_x).reshape(1)

    xs = _dispatch(jnp.concatenate([offs + counts, n_used]), dest.reshape(t // tm_x, 1, tm_x * TOP_K), h2, ns, tm_x)
    ys = _experts(tile_expert, n_used, xs, g, wgu, bgu, wdn, bdn, tm_x)
    return _combine(dest.reshape(t // tm_c, 1, tm_c * TOP_K), prob, h2, gf, ys, tm_c)


def _pick_tile(n, prefs):
    for t in prefs:
        if n % t == 0:
            return t
    return n


def _pad_rows(w, rows, offset=0):
    out = jnp.zeros((rows, w.shape[1]), w.dtype)
    return out.at[offset:offset + w.shape[0]].set(w)


def kernel(x, meta_tokens, norm_mix_g, w_in, w_gla_a2, b_gla_a, gla_norm_g, w_gla_o, mu_r, mu_k, mu_v, mu_w, mu_a, mu_g, w_decay2, b_decay, w_a2, b_a, w_gate2, k_k, k_a, r_k, ln_x_g, ln_x_b, w_rwkv_o, w_out, norm_ffn_g, w_router, b_router, w_exp_gu, b_exp_gu, w_exp_down, b_exp_down, norm_final_g):
    batch, seq, d = x.shape
    assert d == D_MODEL and seq % CHUNK == 0 and w_in.shape[0] == 1
    lp = FRONT_PAD + N_META + seq
    row = lambda a: a.reshape(1, -1)

    meta = jnp.broadcast_to(meta_tokens[None].astype(x.dtype), (batch, N_META, d))
    hp = jnp.concatenate([jnp.zeros((batch, FRONT_PAD, d), x.dtype), meta, x], axis=1).reshape(batch * lp, d)

    splits = (GLA_KEY, GLA_KEY, D_MODEL, D_MODEL, GLA_GATE_RANK, D_MODEL, D_MODEL, D_MODEL,
              RWKV_DECAY_RANK, RWKV_A_RANK, RWKV_GATE_RANK, D_MODEL, D_MODEL)
    offs = [0]
    for s in splits:
        offs.append(offs[-1] + s)
    piece = lambda i: w_in[0][:, offs[i]:offs[i + 1]]
    zcols = lambda n: jnp.zeros((d, n), w_in.dtype)
    small = jnp.concatenate([piece(4), zcols(LANES - GLA_GATE_RANK), piece(8), piece(9), piece(10),
                             zcols(2 * LANES - RWKV_GATE_RANK)], axis=1)
    w_big = jnp.concatenate([piece(0), piece(1), piece(2), piece(3), piece(5), piece(6), piece(7),
                             piece(11), piece(12), small], axis=1).astype(BF16)

    m = batch * lp
    tm = _pick_tile(m, (1280, 640, 512, 320, 256, 192, 128, 64))
    tn = _pick_tile(NP_COLS, (2176, 512))
    p3 = _inproj(hp, row(norm_mix_g[0]), w_big, tm, tn).reshape(batch, lp, NP_COLS)

    mg = _gla(p3, _pad_rows(w_gla_a2[0], LANES).astype(BF16), row(b_gla_a[0]), row(gla_norm_g[0]),
              w_gla_o[0].astype(BF16), _pick_tile(batch, (GLA_SEQS_PER_STEP, 2, 1)))

    pv = lambda a: a.reshape(RWKV_PAIRS, 1, LANES)
    mu_small = jnp.concatenate([jnp.zeros((LANES,), F32), mu_w[0], mu_a[0], mu_g[0],
                                jnp.zeros((2 * LANES - RWKV_GATE_RANK,), F32)])
    prm = (row(mu_r[0]), row(mu_k[0]), row(mu_v[0]), row(mu_small),
           _pad_rows(w_decay2[0], LANES).astype(BF16), row(b_decay[0]),
           _pad_rows(w_a2[0], LANES, RWKV_DECAY_RANK).astype(BF16), row(b_a[0]),
           _pad_rows(w_gate2[0], 2 * LANES).astype(BF16),
           pv(k_k[0]), pv(k_a[0]), pv(r_k[0]), pv(ln_x_g[0]), pv(ln_x_b[0]),
           w_rwkv_o[0].astype(BF16), w_out[0].astype(BF16))
    h2 = _rwkv(p3, mg, x, prm, _pick_tile(batch, (RWKV_SEQS_PER_STEP, 1))).reshape(batch * seq, d)

    wr = jnp.zeros((d, LANES), F32).at[:, :N_EXPERTS].set(w_router[0])
    wr3 = jnp.concatenate(_split3(wr), axis=1)
    br = jnp.zeros((1, LANES), F32).at[0, :N_EXPERTS].set(b_router[0])
    out = _moe(h2, row(norm_ffn_g[0]), wr3, br, w_exp_gu[0], b_exp_gu[0][:, None, :],
               w_exp_down[0], b_exp_down[0][:, None, :], row(norm_final_g))
    return out.reshape(batch, seq, d)
```

```python
import math

import jax
import jax.numpy as jnp
from jax import lax
from jax.experimental import pallas as pl
from jax.experimental.pallas import tpu as pltpu

F32 = jnp.float32
BF16 = jnp.bfloat16

D_MODEL = 1024
N_META = 16
NORM_EPS = 1e-5
CHUNK = 64
FRONT_PAD = (-N_META) % CHUNK
GLA_HEADS = 4
GLA_DK = 128
GLA_DV = 256
GLA_KEY = GLA_HEADS * GLA_DK
GLA_GATE_RANK = 16
GLA_GATE_NORM = 16.0
RWKV_HEAD = 64
RWKV_PAIRS = D_MODEL // (2 * RWKV_HEAD)
RWKV_DECAY_RANK = 64
RWKV_A_RANK = 64
RWKV_GATE_RANK = 160
RWKV_GN_EPS = 64e-5
RWKV_DECAY_SCALE = math.exp(-0.5)
N_EXPERTS = 32
TOP_K = 4
SWIGLU_LIMIT = 7.0
SWIGLU_ALPHA = 1.702
GLA_SEQS_PER_STEP = 8
RWKV_SEQS_PER_STEP = 4
DMA_UNROLL = 8
LANES = 128
SUBLANES = 8
SMALL_W = 512
COL_GQ, COL_GK, COL_GV, COL_GR = 0, 512, 1024, 2048
COL_RR, COL_RK, COL_RV = 3072, 4096, 5120
COL_GATE_GLA, COL_GATE_RWKV, COL_SMALL = 6144, 7168, 8192
NP_COLS = COL_SMALL + SMALL_W
VMEM_LIMIT = 56 * 1024 * 1024


def _dot(a, b):
    return jnp.dot(a, b, preferred_element_type=F32)


def _bmm(a, b):
    return jnp.einsum("gik,gkj->gij", a, b, preferred_element_type=F32)


def _bmm_nt(a, b):
    return jnp.einsum("gik,gjk->gij", a, b, preferred_element_type=F32)


def _bmm_tn(a, b):
    return jnp.einsum("gti,gtj->gij", a, b, preferred_element_type=F32)


def _split2(x):
    hi = x.astype(BF16)
    lo = (x - hi.astype(F32)).astype(BF16)
    return hi, lo


def _split3(x):
    hi = x.astype(BF16)
    r1 = x - hi.astype(F32)
    mid = r1.astype(BF16)
    lo = (r1 - mid.astype(F32)).astype(BF16)
    return hi, mid, lo


def _cumsum_rows(x, tri, split):
    parts = split(x)
    out = _dot(tri, parts[0])
    for p in parts[1:]:
        out = out + _dot(tri, p)
    return out


def _softplus(x):
    return jnp.maximum(x, 0.0) + jnp.log1p(jnp.exp(-jnp.abs(x)))


def _tri_blocks(rows, blk):
    r = lax.broadcasted_iota(jnp.int32, (rows, rows), 0)
    c = lax.broadcasted_iota(jnp.int32, (rows, rows), 1)
    return ((r >= c) & (r // blk == c // blk)).astype(BF16)


def _split_lanes(x, n, w):
    s, c, _ = x.shape
    return jnp.stack([x[:, :, j * w:(j + 1) * w] for j in range(n)], axis=1).reshape(s * n, c, w)


def _merge_lanes(x, n):
    sn, c, w = x.shape
    x = x.reshape(sn // n, n, c, w)
    return jnp.concatenate([x[:, j] for j in range(n)], axis=-1).reshape(sn // n * c, n * w)


def _inproj_body(x_ref, g_ref, w_ref, o_ref, u_scr):
    @pl.when(pl.program_id(1) == 0)
    def _():
        x = x_ref[...]
        ms = jnp.mean(x * x, axis=-1, keepdims=True)
        u_scr[...] = (x * lax.rsqrt(ms + NORM_EPS) * g_ref[...]).astype(BF16)

    o_ref[...] = _dot(u_scr[...], w_ref[...])


def _inproj(hp, g, w, tm, tn):
    m = hp.shape[0]
    return pl.pallas_call(
        _inproj_body,
        grid=(m // tm, NP_COLS // tn),
        in_specs=[
            pl.BlockSpec((tm, D_MODEL), lambda i, j: (i, 0)),
            pl.BlockSpec((1, D_MODEL), lambda i, j: (0, 0)),
            pl.BlockSpec((D_MODEL, tn), lambda i, j: (0, j)),
        ],
        out_specs=pl.BlockSpec((tm, tn), lambda i, j: (i, j)),
        out_shape=jax.ShapeDtypeStruct((m, NP_COLS), F32),
        scratch_shapes=[pltpu.VMEM((tm, D_MODEL), BF16)],
        compiler_params=pltpu.CompilerParams(
            dimension_semantics=("parallel", "arbitrary"), vmem_limit_bytes=VMEM_LIMIT),
        name="inproj",
    )(hp, g, w)


def _gla_body(q_ref, k_ref, v_ref, r_ref, sm_ref, gate_ref, wa2_ref, ba_ref, ng_ref, wo_ref,
              o_ref, s_scr):
    c = pl.program_id(1)
    nb = q_ref.shape[0]
    C = CHUNK
    rows = nb * C
    H = GLA_HEADS

    @pl.when(c == 0)
    def _():
        s_scr[...] = jnp.zeros_like(s_scr)

    row = lax.broadcasted_iota(jnp.int32, (rows, 1), 0) % C
    causal = (lax.broadcasted_iota(jnp.int32, (C, C), 0) >= lax.broadcasted_iota(jnp.int32, (C, C), 1))[None]

    z = _dot(sm_ref[...].reshape(rows, SMALL_W)[:, 0:LANES].astype(BF16), wa2_ref[...]) + ba_ref[...]
    gk = -_softplus(-z) * (1.0 / GLA_GATE_NORM)
    gk = jnp.where((c > 0) | (row >= FRONT_PAD), gk, 0.0)
    b = _cumsum_rows(gk, _tri_blocks(rows, C), _split3).reshape(nb, C, GLA_KEY)
    b_ref = b[:, C // 2 - 1:C // 2, :]
    b_last = b[:, C - 1:C, :]

    q = q_ref[...] * (GLA_DK ** -0.5)
    k = k_ref[...]
    keys = lambda x: _split_lanes(x, H, GLA_DK)
    vals = lambda x: _split_lanes(x, H, GLA_DV)
    qe = keys((q * jnp.exp(b - b_ref)).astype(BF16))
    ke = keys((k * jnp.exp(b_ref - b)).astype(BF16))
    qs = keys((q * jnp.exp(b)).astype(BF16))
    kd = keys((k * jnp.exp(b_last - b)).astype(BF16))
    decay = keys(jnp.exp(b_last))
    v = vals(v_ref[...].astype(BF16))
    silu_r = r_ref[...]
    silu_r = vals(silu_r * jax.nn.sigmoid(silu_r))

    a = jnp.where(causal, _bmm_nt(qe, ke), 0.0)
    st = s_scr[...]
    o = _bmm(a.astype(BF16), v) + _bmm_nt(qs, st.astype(BF16))
    s_scr[...] = st * decay + _bmm_tn(v, kd)
    o = o * lax.rsqrt(jnp.mean(o * o, axis=-1, keepdims=True) + NORM_EPS) * ng_ref[...]
    og = _merge_lanes(o * silu_r, H).astype(BF16)
    out = jax.nn.sigmoid(gate_ref[...].reshape(rows, D_MODEL)) * _dot(og, wo_ref[...])
    o_ref[...] = out.reshape(nb, C, D_MODEL)


def _gla(p3, wa2, ba, ng, wo, nb):
    batch, lp, _ = p3.shape
    n_chunks = lp // CHUNK
    pspec = lambda w, col: pl.BlockSpec((nb, CHUNK, w), lambda b, c: (b, c, col // w))
    full = lambda shape: pl.BlockSpec(shape, lambda b, c: (0,) * len(shape))
    return pl.pallas_call(
        _gla_body,
        grid=(batch // nb, n_chunks),
        in_specs=[
            pspec(GLA_KEY, COL_GQ), pspec(GLA_KEY, COL_GK), pspec(D_MODEL, COL_GV),
            pspec(D_MODEL, COL_GR), pspec(SMALL_W, COL_SMALL), pspec(D_MODEL, COL_GATE_GLA),
            full((LANES, GLA_KEY)), full((1, GLA_KEY)), full((1, GLA_DV)), full((D_MODEL, D_MODEL)),
        ],
        out_specs=pl.BlockSpec((nb, CHUNK, D_MODEL), lambda b, c: (b, jnp.maximum(c - 1, 0), 0)),
        out_shape=jax.ShapeDtypeStruct((batch, (n_chunks - 1) * CHUNK, D_MODEL), F32),
        scratch_shapes=[pltpu.VMEM((nb * GLA_HEADS, GLA_DV, GLA_DK), F32)],
        compiler_params=pltpu.CompilerParams(
            dimension_semantics=("parallel", "arbitrary"), vmem_limit_bytes=VMEM_LIMIT),
        name="gla",
    )(p3, p3, p3, p3, p3, p3, wa2, ba, ng, wo)


def _stack_heads(x, lane_lo):
    zero = jnp.zeros((), x.dtype)
    return jnp.concatenate([jnp.where(lane_lo, x, zero), jnp.where(lane_lo, zero, x)], axis=1)


def _rwkv_body(pr_ref, pk_ref, pv_ref, sm_ref, gate_ref, mg_ref, x_ref,
               mur_ref, muk_ref, muv_ref, musm_ref, wd_ref, bd_ref, wa_ref, ba_ref, wg_ref,
               kk_ref, ka_ref, rk_ref, lng_ref, lnb_ref, wo_ref, wout_ref,
               o_ref,
               shr_scr, shk_scr, shv_scr, shs_scr, s_scr):
    c = pl.program_id(1)
    nb = pr_ref.shape[0]
    C = CHUNK
    rows = nb * C
    G = nb * RWKV_PAIRS

    @pl.when(c == 0)
    def _():
        s_scr[...] = jnp.zeros_like(s_scr)
        for scr in (shr_scr, shk_scr, shv_scr, shs_scr):
            scr[:, SUBLANES - 1:SUBLANES, :] = jnp.zeros((nb, 1, scr.shape[2]), F32)

    def lerp(x_ref_, scr, mu_ref):
        x = x_ref_[...]
        scr[:, SUBLANES:SUBLANES + C, :] = x
        prev = scr[:, SUBLANES - 1:SUBLANES - 1 + C, :]
        scr[:, SUBLANES - 1:SUBLANES, :] = x[:, C - 1:C, :]
        return x + (prev - x) * mu_ref[...]

    pairs = lambda x: _split_lanes(x, RWKV_PAIRS, LANES)
    per_pair = lambda ref: jnp.broadcast_to(ref[...][None], (nb, RWKV_PAIRS, 1, LANES)).reshape(G, 1, LANES)

    r = lerp(pr_ref, shr_scr, mur_ref)
    k = lerp(pk_ref, shk_scr, muk_ref)
    v = lerp(pv_ref, shv_scr, muv_ref)
    ls = lerp(sm_ref, shs_scr, musm_ref).reshape(rows, SMALL_W)
    s1 = ls[:, LANES:2 * LANES]
    xw = bd_ref[...] + _dot(jnp.tanh(s1).astype(BF16), wd_ref[...])
    logw = -RWKV_DECAY_SCALE * jax.nn.sigmoid(xw)
    a = jax.nn.sigmoid(ba_ref[...] + _dot(s1.astype(BF16), wa_ref[...]))
    g = _dot(jax.nn.sigmoid(ls[:, 2 * LANES:4 * LANES]).astype(BF16), wg_ref[...])

    cl = _cumsum_rows(logw, _tri_blocks(rows, C), _split2).reshape(nb, C, D_MODEL)
    logw = logw.reshape(nb, C, D_MODEL)
    cref = cl[:, C // 2 - 1:C // 2, :]
    clast = cl[:, C - 1:C, :]
    e_neg = pairs(jnp.exp(cref - cl))
    e_prev = pairs(jnp.exp(cl - logw - cref))
    e_cur = pairs(jnp.exp(cl - cref))
    g_last = pairs(jnp.exp(clast - cref))
    g_ref = pairs(jnp.exp(cref))
    g_c = pairs(jnp.exp(clast))
    r = pairs(r)
    k = pairs(k)
    v = pairs(v)
    a = pairs(a.reshape(nb, C, D_MODEL))

    lane_lo = lax.broadcasted_iota(jnp.int32, (1, C, LANES), 2) < RWKV_HEAD
    er = lax.broadcasted_iota(jnp.int32, (LANES, LANES), 0)
    ec = lax.broadcasted_iota(jnp.int32, (LANES, LANES), 1)
    seg_ones = ((er // RWKV_HEAD) == (ec // RWKV_HEAD)).astype(BF16)
    strict = ((er % C) > (ec % C))[None]
    incl = ((er % C) >= (ec % C))[None]

    def seg(x):
        return _dot(x.reshape(G * C, LANES).astype(BF16), seg_ones).reshape(G, C, LANES)

    kk = k * per_pair(kk_ref)
    kk = kk * lax.rsqrt(jnp.maximum(seg(kk * kk), 1e-24))
    k2 = k * (1.0 + (a - 1.0) * per_pair(ka_ref))
    bonus = seg(r * k2 * per_pair(rk_ref))
    at = -kk * e_prev
    rt = r * e_cur
    bt = kk * a * e_neg
    kt = k2 * e_neg
    st = s_scr[...]
    sp = (st * g_ref).astype(BF16)

    stk = lambda z: _stack_heads(z.astype(BF16), lane_lo)
    lhs = jnp.concatenate([stk(at), stk(rt)], axis=1)
    rhs = jnp.concatenate([stk(bt), stk(kt)], axis=1)
    m1 = _bmm_nt(lhs, rhs).astype(BF16)
    n2 = 2 * C
    ab = jnp.where(strict, m1[:, :n2, :n2], 0.0)
    ak = jnp.where(strict, m1[:, :n2, n2:], 0.0)
    rb = jnp.where(incl, m1[:, n2:, :n2], 0.0)
    rk = jnp.where(incl, m1[:, n2:, n2:], 0.0)
    vs = stk(v)

    pm = _bmm_nt(lhs[:, :n2], sp) + _bmm(ak, vs)
    x = ab
    for i in range(6):
        pm = pm + _bmm(x, pm.astype(BF16))
        if i < 5:
            x = _bmm(x, x).astype(BF16)
    pmb = pm.astype(BF16)
    y2 = _bmm_nt(lhs[:, n2:], sp) + _bmm(rb, pmb) + _bmm(rk, vs)
    y = y2[:, :C] + y2[:, C:]

    upd_l = jnp.concatenate([pmb, vs], axis=1)
    upd_r = jnp.concatenate([stk(bt * g_last), stk(kt * g_last)], axis=1)
    s_scr[...] = st * g_c + _bmm_tn(upd_l, upd_r)

    mean = seg(y) * (1.0 / RWKV_HEAD)
    yc = y - mean
    var = seg(yc * yc) * (1.0 / RWKV_HEAD)
    y = yc * lax.rsqrt(var + RWKV_GN_EPS) * per_pair(lng_ref) + per_pair(lnb_ref) + bonus * v

    y = _merge_lanes(y, RWKV_PAIRS)
    orw = _dot((y * g).astype(BF16), wo_ref[...])
    mix = mg_ref[...].reshape(rows, D_MODEL) + jax.nn.sigmoid(gate_ref[...].reshape(rows, D_MODEL)) * orw
    out = x_ref[...].reshape(rows, D_MODEL) + _dot(mix.astype(BF16), wout_ref[...])
    o_ref[...] = out.reshape(nb, C, D_MODEL)


def _rwkv(p3, mg, x, prm, nb):
    batch, lp, _ = p3.shape
    n_chunks = lp // CHUNK
    real = lambda b, c: (b, jnp.maximum(c - 1, 0), 0)
    pspec = lambda w, col: pl.BlockSpec((nb, CHUNK, w), lambda b, c: (b, c, col // w))
    full = lambda shape: pl.BlockSpec(shape, lambda b, c: (0,) * len(shape))
    pairvec = full((RWKV_PAIRS, 1, LANES))
    vec = full((1, D_MODEL))
    shift = lambda w: pltpu.VMEM((nb, CHUNK + SUBLANES, w), F32)
    return pl.pallas_call(
        _rwkv_body,
        grid=(batch // nb, n_chunks),
        in_specs=[
            pspec(D_MODEL, COL_RR), pspec(D_MODEL, COL_RK), pspec(D_MODEL, COL_RV),
            pspec(SMALL_W, COL_SMALL), pspec(D_MODEL, COL_GATE_RWKV),
            pl.BlockSpec((nb, CHUNK, D_MODEL), real), pl.BlockSpec((nb, CHUNK, D_MODEL), real),
            vec, vec, vec, full((1, SMALL_W)),
            full((LANES, D_MODEL)), vec, full((LANES, D_MODEL)), vec, full((2 * LANES, D_MODEL)),
            pairvec, pairvec, pairvec, pairvec, pairvec,
            full((D_MODEL, D_MODEL)), full((D_MODEL, D_MODEL)),
        ],
        out_specs=pl.BlockSpec((nb, CHUNK, D_MODEL), real),
        out_shape=jax.ShapeDtypeStruct(x.shape, F32),
        scratch_shapes=[shift(D_MODEL), shift(D_MODEL), shift(D_MODEL), shift(SMALL_W),
                        pltpu.VMEM((nb * RWKV_PAIRS, LANES, LANES), F32)],
        compiler_params=pltpu.CompilerParams(
            dimension_semantics=("parallel", "arbitrary"), vmem_limit_bytes=VMEM_LIMIT),
        name="rwkv",
    )(p3, p3, p3, p3, p3, mg, x, *prm)


def _route_body(h_ref, g_ref, wr_ref, br_ref, idx_ref, prob_ref, rank_ref, cnt_ref, base_scr):
    tm = h_ref.shape[0]

    @pl.when(pl.program_id(0) == 0)
    def _():
        base_scr[...] = jnp.zeros_like(base_scr)

    h = h_ref[...]
    u = h * lax.rsqrt(jnp.mean(h * h, axis=-1, keepdims=True) + NORM_EPS) * g_ref[...]
    hi, mid, lo = _split3(u)
    ph = _dot(hi, wr_ref[...])
    pm = _dot(mid, wr_ref[:, :2 * LANES])
    pl_ = _dot(lo, wr_ref[:, :LANES])
    logits = (ph[:, :LANES] + (ph[:, LANES:2 * LANES] + pm[:, :LANES])
              + (ph[:, 2 * LANES:] + pm[:, LANES:] + pl_)) + br_ref[...]
    lane = lax.broadcasted_iota(jnp.int32, (tm, LANES), 1)
    logits = jnp.where(lane < N_EXPERTS, logits, -jnp.inf)
    idx_out = jnp.zeros((tm, LANES), jnp.int32)
    prob_out = jnp.zeros((tm, LANES), F32)
    denom = jnp.zeros((tm, 1), F32)
    hits = []
    top = None
    for k in range(TOP_K):
        m = jnp.max(logits, axis=-1, keepdims=True)
        idx = jnp.min(jnp.where(logits == m, lane, LANES), axis=-1, keepdims=True)
        hit = lane == idx
        top = m if top is None else top
        w = jnp.exp(m - top)
        idx_out = jnp.where(lane == k, idx, idx_out)
        prob_out = jnp.where(lane == k, w, prob_out)
        denom = denom + w
        hits.append(hit)
        logits = jnp.where(hit, -jnp.inf, logits)
    idx_ref[...] = idx_out
    prob_ref[...] = prob_out / denom

    tot = jnp.zeros((tm, LANES), F32)
    for hit in hits:
        tot = tot + jnp.where(hit, 1.0, 0.0)
    r = lax.broadcasted_iota(jnp.int32, (tm, tm), 0)
    c = lax.broadcasted_iota(jnp.int32, (tm, tm), 1)
    before = _dot((r > c).astype(BF16), tot.astype(BF16)) + base_scr[...]
    rank_out = jnp.zeros((tm, LANES), F32)
    for k, hit in enumerate(hits):
        rk = jnp.sum(jnp.where(hit, before, 0.0), axis=-1, keepdims=True)
        rank_out = jnp.where(lane == k, rk, rank_out)
    rank_ref[...] = rank_out.astype(jnp.int32)
    base_scr[...] += jnp.sum(tot, axis=0, keepdims=True)
    cnt_ref[...] = base_scr[...]


def _route(h2, g, wr3, br, tm):
    t = h2.shape[0]
    full = lambda shape: pl.BlockSpec(shape, lambda i: (0,) * len(shape))
    tile = lambda w: pl.BlockSpec((tm, w), lambda i: (i, 0))
    return pl.pallas_call(
        _route_body,
        grid=(t // tm,),
        in_specs=[tile(D_MODEL), full((1, D_MODEL)), full((D_MODEL, 3 * LANES)), full((1, LANES))],
        out_specs=[tile(LANES), tile(LANES), tile(LANES), full((1, LANES))],
        out_shape=[jax.ShapeDtypeStruct((t, LANES), jnp.int32),
                   jax.ShapeDtypeStruct((t, LANES), F32),
                   jax.ShapeDtypeStruct((t, LANES), jnp.int32),
                   jax.ShapeDtypeStruct((1, LANES), F32)],
        scratch_shapes=[pltpu.VMEM((1, LANES), F32)],
        compiler_params=pltpu.CompilerParams(
            dimension_semantics=("arbitrary",), vmem_limit_bytes=VMEM_LIMIT),
        name="route",
    )(h2, g, wr3, br)


def _dispatch_body(tail_ref, dest_ref, h_ref, xs_ref, hs_scr, sem):
    tm = h_ref.shape[0]
    zrows = hs_scr.shape[0]

    @pl.when(pl.program_id(0) == 0)
    def _():
        hs_scr[...] = jnp.zeros_like(hs_scr)
        for e in range(N_EXPERTS):
            start = pl.multiple_of(tail_ref[e] * SUBLANES, SUBLANES)
            pltpu.make_async_copy(hs_scr, xs_ref.at[pl.ds(start, zrows)], sem).start()
        for e in range(N_EXPERTS):
            pltpu.make_async_copy(hs_scr, xs_ref.at[pl.ds(0, zrows)], sem).wait()

        def zero_tile(j, carry):
            start = pl.multiple_of(j * zrows, zrows)
            pltpu.make_async_copy(hs_scr, xs_ref.at[pl.ds(start, zrows)], sem).start()
            return carry

        def wait_tile(j, carry):
            pltpu.make_async_copy(hs_scr, xs_ref.at[pl.ds(0, zrows)], sem).wait()
            return carry

        n_tiles = xs_ref.shape[0] // zrows
        lax.fori_loop(tail_ref[N_EXPERTS], n_tiles, zero_tile, 0)
        lax.fori_loop(tail_ref[N_EXPERTS], n_tiles, wait_tile, 0)

    for c in range(SUBLANES):
        hs_scr[pl.ds(c, tm, stride=SUBLANES), :] = h_ref[:, c * LANES:(c + 1) * LANES]

    def issue(t, carry):
        src = hs_scr.at[pl.ds(pl.multiple_of(t * SUBLANES, SUBLANES), SUBLANES)]
        for k in range(TOP_K):
            d = pl.multiple_of(dest_ref[0, t * TOP_K + k] * SUBLANES, SUBLANES)
            pltpu.make_async_copy(src, xs_ref.at[pl.ds(d, SUBLANES)], sem).start(priority=k % 2)
        return carry

    lax.fori_loop(0, tm, issue, 0, unroll=DMA_UNROLL)
    for k in range(TOP_K):
        pltpu.make_async_copy(hs_scr, xs_ref.at[pl.ds(0, tm * SUBLANES)], sem).wait()


def _dispatch(tail, dest2, h2, ns, tm):
    t = h2.shape[0]
    return pl.pallas_call(
        _dispatch_body,
        grid_spec=pltpu.PrefetchScalarGridSpec(
            num_scalar_prefetch=1,
            grid=(t // tm,),
            in_specs=[pl.BlockSpec((None, 1, tm * TOP_K), lambda i, tl: (i, 0, 0), memory_space=pltpu.SMEM),
                      pl.BlockSpec((tm, D_MODEL), lambda i, tl: (i, 0))],
            out_specs=pl.BlockSpec(memory_space=pl.ANY),
            scratch_shapes=[pltpu.VMEM((tm * SUBLANES, LANES), F32), pltpu.SemaphoreType.DMA(())],
        ),
        out_shape=jax.ShapeDtypeStruct((ns * SUBLANES, LANES), F32),
        compiler_params=pltpu.CompilerParams(
            dimension_semantics=("arbitrary",), vmem_limit_bytes=VMEM_LIMIT),
        name="dispatch",
    )(tail, dest2, h2)


def _experts_body(te_ref, nu_ref, xs_ref, g_ref, wgu_ref, bgu_ref, wdn_ref, bdn_ref, ys_ref,
                  wgu_scr, wdn_scr):
    i = pl.program_id(0)
    tm = xs_ref.shape[0] // SUBLANES

    @pl.when((i == 0) | (te_ref[i] != te_ref[jnp.maximum(i - 1, 0)]))
    def _():
        wgu_scr[...] = wgu_ref[0].astype(BF16)
        wdn_scr[...] = wdn_ref[0].astype(BF16)

    @pl.when(i < nu_ref[0])
    def _():
        x = jnp.concatenate([xs_ref[pl.ds(c, tm, stride=SUBLANES), :] for c in range(SUBLANES)], axis=1)
        u = x * lax.rsqrt(jnp.mean(x * x, axis=-1, keepdims=True) + NORM_EPS) * g_ref[...]
        hgu = _dot(u.astype(BF16), wgu_scr[...]) + bgu_ref[0]
        d = wdn_scr.shape[0]
        gate = jnp.minimum(hgu[:, :d], SWIGLU_LIMIT)
        up = jnp.clip(hgu[:, d:], -SWIGLU_LIMIT, SWIGLU_LIMIT)
        act = (up + 1.0) * (gate * jax.nn.sigmoid(gate * SWIGLU_ALPHA))
        y = _dot(act.astype(BF16), wdn_scr[...]) + bdn_ref[0]
        for c in range(SUBLANES):
            ys_ref[pl.ds(c, tm, stride=SUBLANES), :] = y[:, c * LANES:(c + 1) * LANES]

    @pl.when(i >= nu_ref[0])
    def _():
        ys_ref[...] = jnp.zeros_like(ys_ref)


def _experts(tile_expert, n_used, xs, g, wgu, bgu, wdn, bdn, tm):
    n_tiles = xs.shape[0] // (tm * SUBLANES)
    wsel = lambda shape: pl.BlockSpec(shape, lambda i, te, nu: (te[i], 0, 0))
    used = lambda i, te, nu: (jnp.minimum(i, nu[0] - 1), 0)
    return pl.pallas_call(
        _experts_body,
        grid_spec=pltpu.PrefetchScalarGridSpec(
            num_scalar_prefetch=2,
            grid=(n_tiles,),
            in_specs=[pl.BlockSpec((tm * SUBLANES, LANES), used),
                      pl.BlockSpec((1, D_MODEL), lambda i, te, nu: (0, 0)),
                      wsel((1, D_MODEL, 2 * D_MODEL)), wsel((1, 1, 2 * D_MODEL)),
                      wsel((1, D_MODEL, D_MODEL)), wsel((1, 1, D_MODEL))],
            out_specs=pl.BlockSpec((tm * SUBLANES, LANES), lambda i, te, nu: (i, 0)),
            scratch_shapes=[pltpu.VMEM((D_MODEL, 2 * D_MODEL), BF16), pltpu.VMEM((D_MODEL, D_MODEL), BF16)],
        ),
        out_shape=jax.ShapeDtypeStruct(xs.shape, F32),
        compiler_params=pltpu.CompilerParams(
            dimension_semantics=("arbitrary",), vmem_limit_bytes=VMEM_LIMIT),
        name="experts",
    )(tile_expert, n_used, xs, g, wgu, bgu, wdn, bdn)


def _combine_body(dcur_ref, dnext_ref, prob_ref, h_ref, gf_ref, ys_ref, o_ref, ybuf, sem):
    i = pl.program_id(0)
    n = pl.num_programs(0)
    tm = h_ref.shape[0]
    slot = i % 2

    def gather(dref, s):
        def issue(t, carry):
            row = pl.multiple_of(t * SUBLANES, SUBLANES)
            for k in range(TOP_K):
                d = pl.multiple_of(dref[0, t * TOP_K + k] * SUBLANES, SUBLANES)
                pltpu.make_async_copy(ys_ref.at[pl.ds(d, SUBLANES)], ybuf.at[s, k, pl.ds(row, SUBLANES)],
                                      sem.at[s]).start(priority=k % 2)
            return carry
        lax.fori_loop(0, tm, issue, 0, unroll=DMA_UNROLL)

    @pl.when(i == 0)
    def _():
        gather(dcur_ref, 0)

    @pl.when(i + 1 < n)
    def _():
        gather(dnext_ref, 1 - slot)

    for k in range(TOP_K):
        pltpu.make_async_copy(ys_ref.at[pl.ds(0, tm * SUBLANES)], ybuf.at[slot, k], sem.at[slot]).wait()
    prob = prob_ref[...]
    cols = []
    for c in range(SUBLANES):
        acc = h_ref[:, c * LANES:(c + 1) * LANES]
        for k in range(TOP_K):
            acc = acc + prob[:, k:k + 1] * ybuf[slot, k, pl.ds(c, tm, stride=SUBLANES), :]
        cols.append(acc)
    h = jnp.concatenate(cols, axis=1)
    o_ref[...] = h * lax.rsqrt(jnp.mean(h * h, axis=-1, keepdims=True) + NORM_EPS) * gf_ref[...]


def _combine(dest2, prob, h2, gf, ys, tm):
    t = h2.shape[0]
    n = t // tm
    smem = lambda imap: pl.BlockSpec((None, 1, tm * TOP_K), imap, memory_space=pltpu.SMEM)
    return pl.pallas_call(
        _combine_body,
        grid=(n,),
        in_specs=[smem(lambda i: (i, 0, 0)), smem(lambda i: (jnp.minimum(i + 1, n - 1), 0, 0)),
                  pl.BlockSpec((tm, LANES), lambda i: (i, 0)),
                  pl.BlockSpec((tm, D_MODEL), lambda i: (i, 0)),
                  pl.BlockSpec((1, D_MODEL), lambda i: (0, 0)),
                  pl.BlockSpec(memory_space=pl.ANY)],
        out_specs=pl.BlockSpec((tm, D_MODEL), lambda i: (i, 0)),
        out_shape=jax.ShapeDtypeStruct((t, D_MODEL), F32),
        scratch_shapes=[pltpu.VMEM((2, TOP_K, tm * SUBLANES, LANES), F32), pltpu.SemaphoreType.DMA((2,))],
        compiler_params=pltpu.CompilerParams(
            dimension_semantics=("arbitrary",), vmem_limit_bytes=VMEM_LIMIT),
        name="combine",
    )(dest2, dest2, prob, h2, gf, ys)


def _moe(h2, g, wr3, br, wgu, bgu, wdn, bdn, gf):
    t = h2.shape[0]
    tm_x = _pick_tile(t, (512, 256, 128))
    tm_c = _pick_tile(t, (256, 128))
    idx, prob, rank, cnt = _route(h2, g, wr3, br, tm_x)

    counts = cnt[0, :N_EXPERTS].astype(jnp.int32)
    padded = (counts + tm_x - 1) // tm_x * tm_x
    ends = jnp.cumsum(padded)
    offs = ends - padded
    ns = t * TOP_K + (N_EXPERTS + 1) * tm_x
    e_ids = jnp.arange(N_EXPERTS, dtype=jnp.int32)
    idx4 = idx[:, :TOP_K]
    dest = rank[:, :TOP_K] + jnp.sum(jnp.where(idx4[..., None] == e_ids, offs, 0), axis=-1)
    tile_start = jnp.arange(ns // tm_x, dtype=jnp.int32) * tm_x
    tile_expert = jnp.minimum(jnp.sum((tile_start[:, None] >= ends[None, :]).astype(jnp.int32), axis=1),
                              N_EXPERTS - 1)
    n_used = (ends[-1] // tm_x).reshape(1)

    xs = _dispatch(jnp.concatenate([offs + counts, n_used]), dest.reshape(t // tm_x, 1, tm_x * TOP_K), h2, ns, tm_x)
    ys = _experts(tile_expert, n_used, xs, g, wgu, bgu, wdn, bdn, tm_x)
    return _combine(dest.reshape(t // tm_c, 1, tm_c * TOP_K), prob, h2, gf, ys, tm_c)


def _pick_tile(n, prefs):
    for t in prefs:
        if n % t == 0:
            return t
    return n


def _pad_rows(w, rows, offset=0):
    out = jnp.zeros((rows, w.shape[1]), w.dtype)
    return out.at[offset:offset + w.shape[0]].set(w)


def kernel(x, meta_tokens, norm_mix_g, w_in, w_gla_a2, b_gla_a, gla_norm_g, w_gla_o, mu_r, mu_k, mu_v, mu_w, mu_a, mu_g, w_decay2, b_decay, w_a2, b_a, w_gate2, k_k, k_a, r_k, ln_x_g, ln_x_b, w_rwkv_o, w_out, norm_ffn_g, w_router, b_router, w_exp_gu, b_exp_gu, w_exp_down, b_exp_down, norm_final_g):
    batch, seq, d = x.shape
    assert d == D_MODEL and seq % CHUNK == 0 and w_in.shape[0] == 1
    lp = FRONT_PAD + N_META + seq
    row = lambda a: a.reshape(1, -1)

    meta = jnp.broadcast_to(meta_tokens[None].astype(x.dtype), (batch, N_META, d))
    hp = jnp.concatenate([jnp.zeros((batch, FRONT_PAD, d), x.dtype), meta, x], axis=1).reshape(batch * lp, d)

    splits = (GLA_KEY, GLA_KEY, D_MODEL, D_MODEL, GLA_GATE_RANK, D_MODEL, D_MODEL, D_MODEL,
              RWKV_DECAY_RANK, RWKV_A_RANK, RWKV_GATE_RANK, D_MODEL, D_MODEL)
    offs = [0]
    for s in splits:
        offs.append(offs[-1] + s)
    piece = lambda i: w_in[0][:, offs[i]:offs[i + 1]]
    zcols = lambda n: jnp.zeros((d, n), w_in.dtype)
    small = jnp.concatenate([piece(4), zcols(LANES - GLA_GATE_RANK), piece(8), piece(9), piece(10),
                             zcols(2 * LANES - RWKV_GATE_RANK)], axis=1)
    w_big = jnp.concatenate([piece(0), piece(1), piece(2), piece(3), piece(5), piece(6), piece(7),
                             piece(11), piece(12), small], axis=1).astype(BF16)

    m = batch * lp
    tm = _pick_tile(m, (1280, 640, 512, 320, 256, 192, 128, 64))
    tn = _pick_tile(NP_COLS, (2176, 512))
    p3 = _inproj(hp, row(norm_mix_g[0]), w_big, tm, tn).reshape(batch, lp, NP_COLS)

    mg = _gla(p3, _pad_rows(w_gla_a2[0], LANES).astype(BF16), row(b_gla_a[0]), row(gla_norm_g[0]),
              w_gla_o[0].astype(BF16), _pick_tile(batch, (GLA_SEQS_PER_STEP, 2, 1)))

    pv = lambda a: a.reshape(RWKV_PAIRS, 1, LANES)
    mu_small = jnp.concatenate([jnp.zeros((LANES,), F32), mu_w[0], mu_a[0], mu_g[0],
                                jnp.zeros((2 * LANES - RWKV_GATE_RANK,), F32)])
    prm = (row(mu_r[0]), row(mu_k[0]), row(mu_v[0]), row(mu_small),
           _pad_rows(w_decay2[0], LANES).astype(BF16), row(b_decay[0]),
           _pad_rows(w_a2[0], LANES, RWKV_DECAY_RANK).astype(BF16), row(b_a[0]),
           _pad_rows(w_gate2[0], 2 * LANES).astype(BF16),
           pv(k_k[0]), pv(k_a[0]), pv(r_k[0]), pv(ln_x_g[0]), pv(ln_x_b[0]),
           w_rwkv_o[0].astype(BF16), w_out[0].astype(BF16))
    h2 = _rwkv(p3, mg, x, prm, _pick_tile(batch, (RWKV_SEQS_PER_STEP, 1))).reshape(batch * seq, d)

    wr = jnp.zeros((d, LANES), F32).at[:, :N_EXPERTS].set(w_router[0])
    wr3 = jnp.concatenate(_split3(wr), axis=1)
    br = jnp.zeros((1, LANES), F32).at[0, :N_EXPERTS].set(b_router[0])
    out = _moe(h2, row(norm_ffn_g[0]), wr3, br, w_exp_gu[0], b_exp_gu[0][:, None, :],
               w_exp_down[0], b_exp_down[0][:, None, :], row(norm_final_g))
    return out.reshape(batch, seq, d)
```
